```python
import math
import jax, jax.numpy as jnp
from jax import lax
import numpy as np

D_MODEL = 1024
BATCH = 32
SEQ = 2048
DEPTH = 1

HEAD_DIM = 64
NSA_HEADS = 8
NSA_KV_GROUPS = 2
NSA_REP = NSA_HEADS // NSA_KV_GROUPS
CMP_BLOCK = 32
CMP_STRIDE = 16
CMP_HIDDEN = 256
SEL_BLOCK = 64
SEL_TOP = 16
WINDOW = 512
Q_BLOCK = 128
NSA_WIDTH = NSA_HEADS * HEAD_DIM
DN_HEADS = 8
DN_CONV = 4
DN_CHUNK = 64
DN_WIDTH = DN_HEADS * HEAD_DIM
N_EXPERTS = 32
TOP_K = 4
D_EXPERT = D_MODEL
SWIGLU_LIMIT = 7.0
SWIGLU_ALPHA = 1.702
N_BRANCH = 2
N_MOD = 6
EPS = 1e-6
MASK_VALUE = -1e30
FORCE_VALUE = 1e9
IN_SPLITS = (NSA_WIDTH, 6 * NSA_KV_GROUPS * HEAD_DIM, 3 * NSA_HEADS,
             3 * DN_WIDTH, DN_HEADS, DN_HEADS, DN_WIDTH, N_BRANCH * D_MODEL)
IN_WIDTH = sum(IN_SPLITS)

kernel_name = 'hybrid_nsa_deltanet_moe_block'


def rms_norm(x, g):
    xf = x.astype(jnp.float32)
    y = xf * lax.rsqrt(jnp.mean(xf * xf, axis=-1, keepdims=True) + EPS)
    return (y * g.astype(jnp.float32)).astype(x.dtype)


def l2_normalize(x):
    xf = x.astype(jnp.float32)
    return xf * lax.rsqrt(jnp.sum(xf * xf, axis=-1, keepdims=True) + EPS)


def masked_softmax(s, mask):
    s = jnp.where(mask, s.astype(jnp.float32), MASK_VALUE)
    p = jax.nn.softmax(s, axis=-1)
    return jnp.where(mask, p, 0.0)


def split_columns(x, sizes):
    out, off = [], 0
    for s in sizes:
        out.append(x[..., off:off + s])
        off += s
    return out


def nsa_mixer(q, kv, gate_logits, pe_k, pe_v, cmp_w1, cmp_b1, cmp_w2):
    B, S = q.shape[:2]
    G, R, Dh = NSA_KV_GROUPS, NSA_REP, HEAD_DIM
    scale = Dh ** -0.5
    qg = q.reshape(B, S, G, R, Dh)
    k_cmp, v_cmp, k_sel, v_sel, k_win, v_win = [kv[:, :, i] for i in range(6)]
    t = jnp.arange(S)

    n_cmp = (S - CMP_BLOCK) // CMP_STRIDE + 1
    cmp_start = jnp.arange(n_cmp) * CMP_STRIDE
    gather_idx = cmp_start[:, None] + jnp.arange(CMP_BLOCK)[None, :]

    def compress(src, pe, w1, b1, w2):
        blk = src[:, gather_idx] + pe[:, None, :]
        blk = blk.transpose(0, 1, 3, 2, 4).reshape(B, n_cmp, G, CMP_BLOCK * Dh)
        return jax.nn.silu(blk @ w1 + b1) @ w2

    kc = compress(k_cmp, pe_k, cmp_w1[0], cmp_b1[0], cmp_w2[0])
    vc = compress(v_cmp, pe_v, cmp_w1[1], cmp_b1[1], cmp_w2[1])
    cmp_mask = (cmp_start + CMP_BLOCK - 1)[None, :] <= t[:, None]
    s_cmp = jnp.einsum('bsgrd,bngd->bgrsn', qg, kc) * scale
    p_cmp = masked_softmax(s_cmp, cmp_mask)
    o_cmp = jnp.einsum('bgrsn,bngd->bsgrd', p_cmp.astype(vc.dtype), vc).reshape(B, S, NSA_HEADS, Dh)

    n_sel = S // SEL_BLOCK
    sel_start = jnp.arange(n_sel) * SEL_BLOCK
    overlap = jnp.clip(jnp.minimum(cmp_start[:, None] + CMP_BLOCK, sel_start[None, :] + SEL_BLOCK)
                       - jnp.maximum(cmp_start[:, None], sel_start[None, :]), 0, None)
    overlap = overlap.astype(jnp.float32) / CMP_BLOCK
    imp = jnp.einsum('bgrsn,nj->bgsj', p_cmp, overlap)
    cur = t // SEL_BLOCK
    j = jnp.arange(n_sel)
    forced = (j[None, :] == 0) | (j[None, :] == cur[:, None]) | (j[None, :] == cur[:, None] - 1)
    causal = sel_start[None, :] <= t[:, None]
    score = jnp.where(forced, FORCE_VALUE, jnp.where(causal, imp, MASK_VALUE))
    n_top = min(SEL_TOP, n_sel)
    _, sel_idx = lax.top_k(score, n_top)

    k_sel_b = k_sel.reshape(B, n_sel, SEL_BLOCK, G, Dh).transpose(0, 3, 1, 2, 4)
    v_sel_b = v_sel.reshape(B, n_sel, SEL_BLOCK, G, Dh).transpose(0, 3, 1, 2, 4)
    pad = ((0, 0), (WINDOW, 0), (0, 0), (0, 0))
    k_win_p = jnp.pad(k_win, pad)
    v_win_p = jnp.pad(v_win, pad)
    b_i = jnp.arange(B)[:, None, None, None]
    g_i = jnp.arange(G)[None, :, None, None]
    n_keys = n_top * SEL_BLOCK

    def query_block(qb):
        start = qb * Q_BLOCK
        tq = start + jnp.arange(Q_BLOCK)
        q_blk = lax.dynamic_slice_in_dim(qg, start, Q_BLOCK, axis=1)
        idx = lax.dynamic_slice_in_dim(sel_idx, start, Q_BLOCK, axis=2)
        kg = k_sel_b[b_i, g_i, idx]
        vg = v_sel_b[b_i, g_i, idx]
        pos = idx[..., None] * SEL_BLOCK + jnp.arange(SEL_BLOCK)
        m_sel = (pos <= tq[:, None, None]).reshape(B, G, 1, Q_BLOCK, n_keys)
        s_sel = jnp.einsum('bqgrd,bgqnld->bgrqnl', q_blk, kg) * scale
        p_sel = masked_softmax(s_sel.reshape(B, G, R, Q_BLOCK, n_keys), m_sel)
        p_sel = p_sel.reshape(B, G, R, Q_BLOCK, n_top, SEL_BLOCK)
        o_sel = jnp.einsum('bgrqnl,bgqnld->bqgrd', p_sel.astype(vg.dtype), vg)
        kw = lax.dynamic_slice_in_dim(k_win_p, start, WINDOW + Q_BLOCK, axis=1)
        vw = lax.dynamic_slice_in_dim(v_win_p, start, WINDOW + Q_BLOCK, axis=1)
        kpos = start - WINDOW + jnp.arange(WINDOW + Q_BLOCK)
        delta = tq[:, None] - kpos[None, :]
        m_win = (delta >= 0) & (delta < WINDOW) & (kpos[None, :] >= 0)
        s_win = jnp.einsum('bqgrd,bkgd->bgrqk', q_blk, kw) * scale
        p_win = masked_softmax(s_win, m_win)
        o_win = jnp.einsum('bgrqk,bkgd->bqgrd', p_win.astype(vw.dtype), vw)
        return o_sel, o_win

    o_sel, o_win = lax.map(query_block, jnp.arange(S // Q_BLOCK))
    o_sel = o_sel.transpose(1, 0, 2, 3, 4, 5).reshape(B, S, NSA_HEADS, Dh)
    o_win = o_win.transpose(1, 0, 2, 3, 4, 5).reshape(B, S, NSA_HEADS, Dh)

    gates = jax.nn.sigmoid(gate_logits)
    o = gates[..., 0:1] * o_cmp + gates[..., 1:2] * o_sel + gates[..., 2:3] * o_win
    return o.reshape(B, S, NSA_WIDTH)


def causal_conv_silu(x, w):
    C = x.shape[-1]
    y = lax.conv_general_dilated(x, w[:, None, :].astype(x.dtype), window_strides=(1,),
                                 padding=[(DN_CONV - 1, 0)],
                                 dimension_numbers=('NWC', 'WIO', 'NWC'),
                                 feature_group_count=C)
    return jax.nn.silu(y)


def gated_delta_net(qkv, beta_logit, a_logit, z, a_log, dt_bias, norm_g):
    B, S, _ = qkv.shape
    H, Dh, C = DN_HEADS, HEAD_DIM, DN_CHUNK
    f32 = jnp.float32
    q, k, v = jnp.split(qkv.astype(f32), 3, axis=-1)
    q = l2_normalize(q.reshape(B, S, H, Dh)) * (Dh ** -0.5)
    k = l2_normalize(k.reshape(B, S, H, Dh))
    v = v.reshape(B, S, H, Dh)
    beta = jax.nn.sigmoid(beta_logit.astype(f32))
    g = -jnp.exp(a_log.astype(f32)) * jax.nn.softplus(a_logit.astype(f32) + dt_bias.astype(f32))
    nc = S // C

    def chunks(a):
        return a.reshape(B, nc, C, H, -1).transpose(1, 0, 3, 2, 4)

    q, k, v = chunks(q), chunks(k), chunks(v)
    beta = chunks(beta[..., None])[..., 0]
    gc = jnp.cumsum(chunks(g[..., None])[..., 0], axis=-1)
    diff = gc[..., :, None] - gc[..., None, :]
    causal = jnp.tril(jnp.ones((C, C), bool))
    strict = jnp.tril(jnp.ones((C, C), bool), -1)
    decay_in = jnp.exp(jnp.where(causal, diff, -jnp.inf))
    kb = k * beta[..., None]
    A = jnp.einsum('nbhid,nbhjd->nbhij', kb, k) * jnp.where(strict, decay_in, 0.0)
    rhs = jnp.concatenate([v * beta[..., None], kb * jnp.exp(gc)[..., None]], axis=-1)
    sol = lax.linalg.triangular_solve(A, rhs, left_side=True, lower=True, unit_diagonal=True)
    u, w = sol[..., :Dh], sol[..., Dh:]
    attn = jnp.einsum('nbhid,nbhjd->nbhij', q, k) * decay_in
    q_dec = q * jnp.exp(gc)[..., None]
    g_last = gc[..., -1]
    k_dec = k * jnp.exp(g_last[..., None] - gc)[..., None]

    def step(state, xs):
        w_c, u_c, q_c, k_c, a_c, gl = xs
        v_new = u_c - jnp.einsum('bhcd,bhde->bhce', w_c, state)
        o_c = jnp.einsum('bhcd,bhde->bhce', q_c, state) + jnp.einsum('bhij,bhje->bhie', a_c, v_new)
        state = state * jnp.exp(gl)[..., None, None] + jnp.einsum('bhcd,bhce->bhde', k_c, v_new)
        return state, o_c

    s0 = jnp.zeros((B, H, Dh, Dh), f32)
    _, o = lax.scan(step, s0, (w, u, q_dec, k_dec, attn, g_last))
    o = o.transpose(1, 0, 3, 2, 4).reshape(B, S, H, Dh)
    o = rms_norm(o, norm_g) * jax.nn.silu(z.reshape(B, S, H, Dh).astype(f32))
    return o.reshape(B, S, DN_WIDTH).astype(qkv.dtype)


def moe_ffn(h, router_w, router_b, w1, b1, w2, b2):
    B, S, D = h.shape
    hf = h.reshape(B * S, D)
    logits = (hf @ router_w + router_b).astype(jnp.float32)
    top_val, top_idx = lax.top_k(logits, TOP_K)
    top_w = jax.nn.softmax(top_val, axis=-1)
    combine = jnp.einsum('tk,tke->te', top_w,
                         jax.nn.one_hot(top_idx, N_EXPERTS, dtype=jnp.float32)).astype(h.dtype)
    out = jnp.zeros_like(hf)
    for e in range(N_EXPERTS):
        u = hf @ w1[e] + b1[e]
        x_glu = jnp.minimum(u[:, :D_EXPERT], SWIGLU_LIMIT)
        x_lin = jnp.clip(u[:, D_EXPERT:], -SWIGLU_LIMIT, SWIGLU_LIMIT)
        act = x_glu * jax.nn.sigmoid(SWIGLU_ALPHA * x_glu) * (x_lin + 1.0)
        out = out + combine[:, e:e + 1] * (act @ w2[e] + b2[e])
    return out.reshape(B, S, D)


def setup_inputs(seed: int = 0) -> dict:
    key = jax.random.key(seed)
    ks = jax.random.split(key, 32)
    L, D, E, F = DEPTH, D_MODEL, N_EXPERTS, D_EXPERT
    nrm = jax.random.normal
    f32 = jnp.float32
    dt = jnp.exp(jax.random.uniform(ks[12], (L, DN_HEADS), f32, math.log(1e-3), math.log(1e-1)))
    return {
        'x': nrm(ks[0], (BATCH, SEQ, D), f32),
        'c': nrm(ks[1], (BATCH, D), f32),
        'w_ada': nrm(ks[2], (L, D, N_MOD * D), f32) * (0.5 * D ** -0.5),
        'b_ada': nrm(ks[3], (L, N_MOD * D), f32) * 0.01,
        'g_norm_mix': 1.0 + 0.05 * nrm(ks[4], (L, D), f32),
        'w_in': nrm(ks[5], (L, D, IN_WIDTH), f32) * D ** -0.5,
        'cmp_pe_k': nrm(ks[6], (L, CMP_BLOCK, HEAD_DIM), f32) * 0.1,
        'cmp_pe_v': nrm(ks[7], (L, CMP_BLOCK, HEAD_DIM), f32) * 0.1,
        'cmp_w1': nrm(ks[8], (L, 2, CMP_BLOCK * HEAD_DIM, CMP_HIDDEN), f32) * (CMP_BLOCK * HEAD_DIM) ** -0.5,
        'cmp_b1': nrm(ks[9], (L, 2, CMP_HIDDEN), f32) * 0.01,
        'cmp_w2': nrm(ks[10], (L, 2, CMP_HIDDEN, HEAD_DIM), f32) * CMP_HIDDEN ** -0.5,
        'dn_conv_w': nrm(ks[11], (L, DN_CONV, 3 * DN_WIDTH), f32) * DN_CONV ** -0.5,
        'dn_a_log': jnp.log(jax.random.uniform(ks[13], (L, DN_HEADS), f32, 1.0, 16.0)),
        'dn_dt_bias': dt + jnp.log(-jnp.expm1(-dt)),
        'dn_norm_g': 1.0 + 0.05 * nrm(ks[14], (L, HEAD_DIM), f32),
        'w_branch': nrm(ks[15], (L, N_BRANCH, NSA_WIDTH, D), f32) * NSA_WIDTH ** -0.5,
        'w_out': nrm(ks[16], (L, D, D), f32) * D ** -0.5,
        'g_norm_ffn': 1.0 + 0.05 * nrm(ks[17], (L, D), f32),
        'router_w': nrm(ks[18], (L, D, E), f32) * D ** -0.5,
        'router_b': nrm(ks[19], (L, E), f32) * 0.01,
        'exp_w1': nrm(ks[20], (L, E, D, 2 * F), f32) * D ** -0.5,
        'exp_b1': nrm(ks[21], (L, E, 2 * F), f32) * 0.01,
        'exp_w2': nrm(ks[22], (L, E, F, D), f32) * F ** -0.5,
        'exp_b2': nrm(ks[23], (L, E, D), f32) * 0.01,
        'final_norm_g': 1.0 + 0.05 * nrm(ks[24], (D,), f32),
    }


def reference(x, c, w_ada, b_ada, g_norm_mix, w_in, cmp_pe_k, cmp_pe_v, cmp_w1, cmp_b1, cmp_w2,
              dn_conv_w, dn_a_log, dn_dt_bias, dn_norm_g, w_branch, w_out, g_norm_ffn,
              router_w, router_b, exp_w1, exp_b1, exp_w2, exp_b2, final_norm_g):
    B, S, D = x.shape
    for l in range(DEPTH):
        mod = jax.nn.silu(c) @ w_ada[l] + b_ada[l]
        shift_a, scale_a, gate_a, shift_f, scale_f, gate_f = [m[:, None, :] for m in jnp.split(mod, N_MOD, axis=-1)]

        h = rms_norm(x, g_norm_mix[l]) * (1.0 + scale_a) + shift_a
        proj = h @ w_in[l]
        nsa_q, nsa_kv, nsa_gate, dn_qkv, dn_beta, dn_a, dn_z, merge_logit = split_columns(proj, IN_SPLITS)
        y_nsa = nsa_mixer(nsa_q.reshape(B, S, NSA_HEADS, HEAD_DIM),
                          nsa_kv.reshape(B, S, 6, NSA_KV_GROUPS, HEAD_DIM),
                          nsa_gate.reshape(B, S, NSA_HEADS, 3),
                          cmp_pe_k[l], cmp_pe_v[l], cmp_w1[l], cmp_b1[l], cmp_w2[l])
        y_dn = gated_delta_net(causal_conv_silu(dn_qkv, dn_conv_w[l]), dn_beta, dn_a, dn_z,
                               dn_a_log[l], dn_dt_bias[l], dn_norm_g[l])
        branches = jnp.einsum('bsgc,gcd->bsgd', jnp.stack([y_nsa, y_dn], axis=2), w_branch[l])
        merge = jax.nn.sigmoid(merge_logit.reshape(B, S, N_BRANCH, D))
        mix = jnp.sum(merge * branches, axis=2) @ w_out[l]
        x = x + gate_a * mix

        h = rms_norm(x, g_norm_ffn[l]) * (1.0 + scale_f) + shift_f
        x = x + gate_f * moe_ffn(h, router_w[l], router_b[l], exp_w1[l], exp_b1[l], exp_w2[l], exp_b2[l])
    return rms_norm(x, final_norm_g)
```

```python
import functools
import math

import numpy as np
import jax
import jax.numpy as jnp
from jax import lax
from jax.experimental import pallas as pl
from jax.experimental.pallas import tpu as pltpu

F32 = jnp.float32
BF16 = jnp.bfloat16

HEAD_DIM = 64
NSA_HEADS = 8
NSA_GROUPS = 2
NSA_REP = NSA_HEADS // NSA_GROUPS
CMP_BLOCK = 32
CMP_STRIDE = 16
CMP_HIDDEN = 256
SEL_BLOCK = 64
SEL_TOP = 16
WINDOW = 512
Q_BLOCK = 128
DN_HEADS = 8
DN_CONV = 4
DN_CHUNK = 64
N_EXPERTS = 32
TOP_K = 4
SWIGLU_LIMIT = 7.0
SWIGLU_ALPHA = 1.702
EPS = 1e-6
MASK_VALUE = -1e30
FORCE_VALUE = 1e9
PAD_SCORE = -3e38

LANES = 128
VMEM_LIMIT = 56 * 2 ** 20

SEL_KC = 256
WIN_KEYS = WINDOW + Q_BLOCK
EXPERT_TILE = 512
SCATTER_TILE = 256

SM_GATE, SM_BETA, SM_A = 0, 24, 32
RT_IDX, RT_W, RT_RANK = 0, 4, 8


def _dot(a, b):
    return jnp.dot(a, b, preferred_element_type=F32)


def _dot_nt(a, b):
    return lax.dot_general(a, b, (((1,), (1,)), ((), ())), preferred_element_type=F32)


def _split3(x):
    hi = x.astype(BF16)
    r1 = x - hi.astype(F32)
    mid = r1.astype(BF16)
    lo = (r1 - mid.astype(F32)).astype(BF16)
    return hi, mid, lo


def _dot_f32_lhs(x, w_bf16):
    hi, mid, lo = _split3(x)
    return _dot(hi, w_bf16) + _dot(mid, w_bf16) + _dot(lo, w_bf16)


def _dot_f32_rhs(w_bf16, x):
    hi, mid, lo = _split3(x)
    return _dot(w_bf16, hi) + _dot(w_bf16, mid) + _dot(w_bf16, lo)


def _sigmoid(x):
    return 1.0 / (1.0 + jnp.exp(-x))


def _params(*sem):
    return pltpu.CompilerParams(dimension_semantics=sem, vmem_limit_bytes=VMEM_LIMIT)


def _ada_body(c_ref, w_ref, b_ref, o_ref):
    c = c_ref[...]
    a = (c * _sigmoid(c)).astype(BF16)
    o_ref[...] = _dot(a, w_ref[...].astype(BF16)) + b_ref[...]


def _ada(c, w, b):
    bsz, d = c.shape
    n = w.shape[1]
    tn = 1024
    return pl.pallas_call(
        _ada_body,
        grid=(n // tn,),
        in_specs=[pl.BlockSpec((bsz, d), lambda j: (0, 0)),
                  pl.BlockSpec((d, tn), lambda j: (0, j)),
                  pl.BlockSpec((1, tn), lambda j: (0, j))],
        out_specs=pl.BlockSpec((bsz, tn), lambda j: (0, j)),
        out_shape=jax.ShapeDtypeStruct((bsz, n), F32),
        compiler_params=_params("arbitrary"),
        name="ada",
    )(c, w, b.reshape(1, n))


IN_SEGS = (("q", 512, BF16), ("kv", 768, BF16), ("dn", 1536, BF16),
           ("z", 512, BF16), ("mg", 2048, BF16), ("sm", LANES, F32))


def _in_columns():
    q0 = 0
    kv0 = q0 + 512
    gate0 = kv0 + 768
    dn0 = gate0 + 24
    beta0 = dn0 + 1536
    a0 = beta0 + 8
    z0 = a0 + 8
    mg0 = z0 + 512
    cols = list(range(q0, q0 + 512))
    for g in range(NSA_GROUPS):
        for i in range(6):
            base = kv0 + i * NSA_GROUPS * HEAD_DIM + g * HEAD_DIM
            cols += list(range(base, base + HEAD_DIM))
    cols += list(range(dn0, dn0 + 1536))
    cols += list(range(z0, z0 + 512))
    cols += list(range(mg0, mg0 + 2048))
    small = list(range(gate0, gate0 + 24)) + list(range(beta0, beta0 + 8)) + list(range(a0, a0 + 8))
    cols += small + [-1] * (LANES - len(small))
    return np.asarray(cols, np.int32)


def _rms_mod(x, g, shift, scale):
    ms = jnp.mean(x * x, axis=-1, keepdims=True)
    y = x * lax.rsqrt(ms + EPS) * g
    return y * (1.0 + scale) + shift


def _inproj_body(x_ref, mod_ref, g_ref, w_ref, *out_refs):
    h = _rms_mod(x_ref[...], g_ref[...], mod_ref[0:1, :], mod_ref[1:2, :])
    hb = h.astype(BF16)
    off = 0
    for ref, (_, width, _) in zip(out_refs, IN_SEGS):
        for c0 in range(0, width, 512):
            cw = min(512, width - c0)
            ref[:, c0:c0 + cw] = _dot(hb, w_ref[:, off + c0:off + c0 + cw]).astype(ref.dtype)
        off += width


def _inproj(x2, mod3, g, w_big, seq):
    t, d = x2.shape
    tm = 512
    per_b = seq // tm
    nw = w_big.shape[1]
    return pl.pallas_call(
        _inproj_body,
        grid=(t // tm,),
        in_specs=[pl.BlockSpec((tm, d), lambda i: (i, 0)),
                  pl.BlockSpec((None, 6, d), lambda i: (i // per_b, 0, 0)),
                  pl.BlockSpec((1, d), lambda i: (0, 0)),
                  pl.BlockSpec((d, nw), lambda i: (0, 0))],
        out_specs=[pl.BlockSpec((tm, w), lambda i: (i, 0)) for _, w, _ in IN_SEGS],
        out_shape=[jax.ShapeDtypeStruct((t, w), dt) for _, w, dt in IN_SEGS],
        compiler_params=_params("arbitrary"),
        name="inproj",
    )(x2, mod3, g.reshape(1, d), w_big)


def _cmp_body(src_ref, w1_ref, pe_ref, b1_ref, w2_ref, o_ref):
    half = CMP_STRIDE * HEAD_DIM
    out = None
    for kind in range(2):
        x = src_ref[kind]
        first = _dot(x, w1_ref[kind, :half, :])
        second = _dot(x, w1_ref[kind, half:, :])
        n = second.shape[0]
        second = pltpu.roll(second, n - 1, 0)
        pew = _dot(pe_ref[kind], w1_ref[kind])[0:1, :]
        pre = first + second + pew + b1_ref[kind]
        hid = (pre * _sigmoid(pre)).astype(BF16)
        term = _dot(hid, w2_ref[kind])
        out = term if out is None else out + term
    o_ref[...] = out.astype(o_ref.dtype)


def _cmp(src, w1, pe, b1, w2p):
    bsz, ng, _, nrow, width = src.shape
    return pl.pallas_call(
        _cmp_body,
        grid=(bsz, ng),
        in_specs=[pl.BlockSpec((None, None, 2, nrow, width), lambda b, g: (b, g, 0, 0, 0)),
                  pl.BlockSpec(w1.shape, lambda b, g: (0, 0, 0)),
                  pl.BlockSpec(pe.shape, lambda b, g: (0, 0, 0)),
                  pl.BlockSpec(b1.shape, lambda b, g: (0, 0, 0)),
                  pl.BlockSpec(w2p.shape, lambda b, g: (0, 0, 0))],
        out_specs=pl.BlockSpec((None, None, nrow, LANES), lambda b, g: (b, g, 0, 0)),
        out_shape=jax.ShapeDtypeStruct((bsz, ng, nrow, LANES), BF16),
        compiler_params=_params("arbitrary", "arbitrary"),
        name="cmp",
    )(src, w1, pe, b1, w2p)


def _nsa_consts(seq):
    scale = HEAD_DIM ** -0.5
    psel = np.zeros((NSA_REP, NSA_REP * HEAD_DIM, LANES), np.float32)
    qout = np.zeros((NSA_REP, LANES, NSA_REP * HEAD_DIM), np.float32)
    for r in range(NSA_REP):
        for d in range(HEAD_DIM):
            psel[r, r * HEAD_DIM + d, d] = scale
            qout[r, HEAD_DIM + d, r * HEAD_DIM + d] = 1.0
    n_cmp = (seq - CMP_BLOCK) // CMP_STRIDE + 1
    n_sel = seq // SEL_BLOCK
    cs = np.arange(n_cmp)[:, None] * CMP_STRIDE
    ss = np.arange(n_sel)[None, :] * SEL_BLOCK
    ov = np.clip(np.minimum(cs + CMP_BLOCK, ss + SEL_BLOCK) - np.maximum(cs, ss), 0, None) / CMP_BLOCK
    overlap = np.zeros((LANES, LANES), np.float32)
    overlap[:n_cmp, :n_sel] = ov
    nchunk = seq // SEL_KC
    pos = np.arange(seq).reshape(nchunk, 1, SEL_KC)
    expand = (pos // SEL_BLOCK == np.arange(LANES).reshape(1, LANES, 1)).astype(np.float32)
    return (jnp.asarray(psel, BF16), jnp.asarray(qout, BF16), jnp.asarray(overlap, BF16),
            jnp.asarray(expand, BF16))


def _masked_softmax(s, allowed):
    s = jnp.where(allowed, s, MASK_VALUE)
    e = jnp.exp(s - jnp.max(s, axis=-1, keepdims=True))
    p = e / jnp.sum(e, axis=-1, keepdims=True)
    return jnp.where(allowed, p, 0.0)


def _nsa_body(q_ref, kv_ref, kcvc_ref, gate_ref, psel_ref, qout_ref, ov_ref, ex_ref, o_ref,
              m_ref, l_ref, acc_ref, *, n_top, n_sel):
    qb = pl.program_id(2)
    rep, qn = NSA_REP, Q_BLOCK
    q2 = q_ref[...]
    qs = jnp.concatenate([_dot(q2, psel_ref[r]) for r in range(rep)], axis=0).astype(BF16)
    tq = qb * qn + lax.broadcasted_iota(jnp.int32, (qn, 1), 0)
    lane = lax.broadcasted_iota(jnp.int32, (1, LANES), 1)

    kcvc = kcvc_ref[...]
    s = _dot_nt(qs, kcvc).reshape(rep, qn, LANES)
    cmask = (lane * CMP_STRIDE + (CMP_BLOCK - 1)) <= tq
    p = _masked_softmax(s, cmask[None])
    o_cmp = _dot(p.reshape(rep * qn, LANES).astype(BF16), kcvc)
    imp = _dot_f32_lhs(jnp.sum(p, axis=0), ov_ref[...])

    cur = tq // SEL_BLOCK
    forced = (lane == 0) | (lane == cur) | (lane == cur - 1)
    causal = lane * SEL_BLOCK <= tq
    score = jnp.where(forced, FORCE_VALUE, jnp.where(causal, imp, MASK_VALUE))
    score = jnp.where(lane < n_sel, score, PAD_SCORE)
    cnt = jnp.zeros((qn, LANES), F32)
    for jp in range(n_sel):
        col = score[:, jp:jp + 1]
        earlier = jnp.where(lane > jp, 1.0, 0.0)
        cnt = cnt + jnp.where(col > score, 1.0, jnp.where(col == score, earlier, 0.0))
    sel = jnp.where(cnt < n_top, 1.0, 0.0).astype(BF16)

    m_ref[...] = jnp.full(m_ref.shape, MASK_VALUE, F32)
    l_ref[...] = jnp.zeros(l_ref.shape, F32)
    acc_ref[...] = jnp.zeros(acc_ref.shape, F32)

    def sel_chunk(c, carry):
        start = pl.multiple_of(c * SEL_KC, SEL_KC)
        kblk = kv_ref[pl.ds(start, SEL_KC), LANES:2 * LANES]
        sc = _dot_nt(qs, kblk).reshape(rep, qn, SEL_KC)
        selm = _dot(sel, ex_ref[c])
        kpos = start + lax.broadcasted_iota(jnp.int32, (1, SEL_KC), 1)
        allowed = jnp.where(kpos <= tq, selm, 0.0) > 0.5
        sc = jnp.where(allowed[None], sc, MASK_VALUE)
        m_old = m_ref[...]
        m_new = jnp.maximum(m_old, jnp.max(sc, axis=-1, keepdims=True))
        alpha = jnp.exp(m_old - m_new)
        pc = jnp.where(allowed[None], jnp.exp(sc - m_new), 0.0)
        l_ref[...] = alpha * l_ref[...] + jnp.sum(pc, axis=-1, keepdims=True)
        pv = _dot(pc.reshape(rep * qn, SEL_KC).astype(BF16), kblk)
        acc_ref[...] = alpha * acc_ref[...] + pv.reshape(rep, qn, LANES)
        m_ref[...] = m_new
        return carry

    n_chunks = (qb * qn + qn + SEL_KC - 1) // SEL_KC
    lax.fori_loop(0, n_chunks, sel_chunk, 0)
    o_sel = acc_ref[...] / l_ref[...]

    wstart = pl.multiple_of(jnp.maximum(qb - WINDOW // qn, 0) * qn, qn)
    kblk = kv_ref[pl.ds(wstart, WIN_KEYS), 2 * LANES:3 * LANES]
    sw = _dot_nt(qs, kblk).reshape(rep, qn, WIN_KEYS)
    delta = tq - (wstart + lax.broadcasted_iota(jnp.int32, (1, WIN_KEYS), 1))
    wmask = (delta >= 0) & (delta < WINDOW)
    pw = _masked_softmax(sw, wmask[None])
    o_win = _dot(pw.reshape(rep * qn, WIN_KEYS).astype(BF16), kblk).reshape(rep, qn, LANES)

    gs = _sigmoid(gate_ref[...])
    o_cmp = o_cmp.reshape(rep, qn, LANES)
    out = None
    for r in range(rep):
        o_r = (gs[:, 3 * r:3 * r + 1] * o_cmp[r] + gs[:, 3 * r + 1:3 * r + 2] * o_sel[r]
               + gs[:, 3 * r + 2:3 * r + 3] * o_win[r])
        term = _dot(o_r.astype(BF16), qout_ref[r])
        out = term if out is None else out + term
    o_ref[...] = out.astype(o_ref.dtype)


def _nsa(q, kv, kcvc, gates, seq):
    t = q.shape[0]
    bsz = t // seq
    nqb = seq // Q_BLOCK
    n_sel = seq // SEL_BLOCK
    n_top = min(SEL_TOP, n_sel)
    assert seq // CMP_STRIDE == LANES and seq >= WIN_KEYS and n_sel <= LANES
    psel, qout, overlap, expand = _nsa_consts(seq)
    gw = NSA_REP * HEAD_DIM
    body = functools.partial(_nsa_body, n_top=n_top, n_sel=n_sel)
    return pl.pallas_call(
        body,
        grid=(bsz, NSA_GROUPS, nqb),
        in_specs=[pl.BlockSpec((Q_BLOCK, gw), lambda b, g, i: (b * nqb + i, g)),
                  pl.BlockSpec((seq, 3 * LANES), lambda b, g, i: (b, g)),
                  pl.BlockSpec((None, None, LANES, LANES), lambda b, g, i: (b, g, 0, 0)),
                  pl.BlockSpec((None, None, Q_BLOCK, 3 * NSA_REP), lambda b, g, i: (b, g, i, 0)),
                  pl.BlockSpec(psel.shape, lambda b, g, i: (0, 0, 0)),
                  pl.BlockSpec(qout.shape, lambda b, g, i: (0, 0, 0)),
                  pl.BlockSpec(overlap.shape, lambda b, g, i: (0, 0)),
                  pl.BlockSpec(expand.shape, lambda b, g, i: (0, 0, 0))],
        out_specs=pl.BlockSpec((Q_BLOCK, gw), lambda b, g, i: (b * nqb + i, g)),
        out_shape=jax.ShapeDtypeStruct((t, NSA_HEADS * HEAD_DIM), BF16),
        scratch_shapes=[pltpu.VMEM((NSA_REP, Q_BLOCK, 1), F32),
                        pltpu.VMEM((NSA_REP, Q_BLOCK, 1), F32),
                        pltpu.VMEM((NSA_REP, Q_BLOCK, LANES), F32)],
        compiler_params=_params("arbitrary", "arbitrary", "arbitrary"),
        name="nsa",
    )(q, kv, kcvc, gates, psel, qout, overlap, expand)


def _dn_body(qkv_ref, sm_ref, z_ref, cw_ref, hp_ref, ng_ref, ones_ref, tri_ref, o_ref,
             state_ref, prev_ref):
    c = pl.program_id(1)
    ch, hd = DN_CHUNK, HEAD_DIM
    width = DN_HEADS * hd

    @pl.when(c == 0)
    def _():
        state_ref[...] = jnp.zeros(state_ref.shape, F32)
        prev_ref[...] = jnp.zeros(prev_ref.shape, F32)

    x = qkv_ref[...].astype(F32)
    xe = jnp.concatenate([prev_ref[...], x], axis=0)
    y = cw_ref[DN_CONV - 1:DN_CONV, :] * x
    for j in range(DN_CONV - 1):
        y = y + cw_ref[j:j + 1, :] * pltpu.roll(xe, DN_CONV - 1 - j, 0)[8:, :]
    prev_ref[...] = x[ch - 8:, :]
    y = y * _sigmoid(y)

    sm = sm_ref[...]
    beta_all = _sigmoid(sm)
    xa = sm + hp_ref[1:2, :]
    softplus = jnp.maximum(xa, 0.0) + jnp.log(1.0 + jnp.exp(-jnp.abs(xa)))
    g_all = -jnp.exp(hp_ref[0:1, :]) * softplus
    gc_all = _dot_f32_rhs(tri_ref[...], g_all)

    lane = lax.broadcasted_iota(jnp.int32, (1, LANES), 1)
    first = lane < hd
    ri = lax.broadcasted_iota(jnp.int32, (2 * ch, 1), 0)
    ci = lax.broadcasted_iota(jnp.int32, (1, 2 * ch), 1)
    same = (ri // ch) == (ci // ch)
    causal = same & ((ri % ch) >= (ci % ch))
    strict = same & ((ri % ch) > (ci % ch))
    blockdiag = (lax.broadcasted_iota(jnp.int32, (LANES, 1), 0) // hd) == (lane // hd)
    ones_blk = ones_ref[...]

    def stack(v):
        return jnp.concatenate([jnp.where(first, v, 0.0), jnp.where(first, 0.0, v)], axis=0)

    def fold(v):
        return v[:ch] + v[ch:]

    def head_sumsq(v):
        hi = (v * v).astype(BF16)
        lo = (v * v - hi.astype(F32)).astype(BF16)
        return _dot(hi, ones_blk) + _dot(lo, ones_blk)

    for p in range(DN_HEADS // 2):
        h0, h1 = 2 * p, 2 * p + 1
        sl = slice(p * LANES, (p + 1) * LANES)

        def bc(tile, base):
            return jnp.where(first, tile[:, base + h0:base + h0 + 1], tile[:, base + h1:base + h1 + 1])

        qp = y[:, sl]
        kp = y[:, width + p * LANES:width + (p + 1) * LANES]
        vp = y[:, 2 * width + p * LANES:2 * width + (p + 1) * LANES]
        qn = qp * lax.rsqrt(head_sumsq(qp) + EPS) * (hd ** -0.5)
        kn = kp * lax.rsqrt(head_sumsq(kp) + EPS)
        beta = bc(beta_all, SM_BETA)
        gc = bc(gc_all, SM_A)
        egc = jnp.exp(gc)
        glast = gc[ch - 1:ch, :]
        kb = kn * beta
        vb = vp * beta

        gcol = jnp.concatenate([gc_all[:, SM_A + h0:SM_A + h0 + 1],
                                gc_all[:, SM_A + h1:SM_A + h1 + 1]], axis=0)
        gmat = jnp.broadcast_to(gcol, (2 * ch, 2 * ch))
        diff = gmat - gmat.T
        dec = jnp.where(causal, jnp.exp(jnp.where(causal, diff, 0.0)), 0.0)

        ks = jnp.concatenate([kn, kn], axis=0).astype(BF16)
        a_mat = _dot_nt(stack(kb).astype(BF16), ks) * jnp.where(strict, dec, 0.0)
        attn = _dot_nt(stack(qn).astype(BF16), ks) * dec

        nmat = -a_mat
        sol = jnp.concatenate([stack(vb), stack(kb * egc)], axis=1)
        sol = sol + _dot(nmat.astype(BF16), sol.astype(BF16))
        for _ in range(int(math.log2(ch)) - 1):
            nmat = _dot(nmat.astype(BF16), nmat.astype(BF16))
            sol = sol + _dot(nmat.astype(BF16), sol.astype(BF16))
        u = fold(sol[:, :LANES])
        w = fold(sol[:, LANES:])

        st = state_ref[p]
        stb = st.astype(BF16)
        v_new = u - _dot(w.astype(BF16), stb)
        o = _dot((qn * egc).astype(BF16), stb) + fold(_dot(attn.astype(BF16), stack(v_new).astype(BF16)))
        k_dec = kn * jnp.exp(glast - gc)
        upd = _dot(k_dec.T.astype(BF16), v_new.astype(BF16))
        state_ref[p] = st * jnp.exp(glast) + jnp.where(blockdiag, upd, 0.0)

        on = o * lax.rsqrt(head_sumsq(o) * (1.0 / hd) + EPS) * ng_ref[...]
        zp = z_ref[:, sl].astype(F32)
        o_ref[:, sl] = (on * (zp * _sigmoid(zp))).astype(o_ref.dtype)


def _dn(qkv, sm, z, conv_w, hp, ng, seq):
    t = qkv.shape[0]
    bsz = t // seq
    nc = seq // DN_CHUNK
    width = DN_HEADS * HEAD_DIM
    ones_blk = jnp.asarray(np.kron(np.eye(2), np.ones((HEAD_DIM, HEAD_DIM))), BF16)
    tri = jnp.asarray(np.tril(np.ones((DN_CHUNK, DN_CHUNK))), BF16)
    return pl.pallas_call(
        _dn_body,
        grid=(bsz, nc),
        in_specs=[pl.BlockSpec((DN_CHUNK, 3 * width), lambda b, c: (b * nc + c, 0)),
                  pl.BlockSpec((DN_CHUNK, LANES), lambda b, c: (b * nc + c, 0)),
                  pl.BlockSpec((DN_CHUNK, width), lambda b, c: (b * nc + c, 0)),
                  pl.BlockSpec(conv_w.shape, lambda b, c: (0, 0)),
                  pl.BlockSpec(hp.shape, lambda b, c: (0, 0)),
                  pl.BlockSpec(ng.shape, lambda b, c: (0, 0)),
                  pl.BlockSpec(ones_blk.shape, lambda b, c: (0, 0)),
                  pl.BlockSpec(tri.shape, lambda b, c: (0, 0))],
        out_specs=pl.BlockSpec((DN_CHUNK, width), lambda b, c: (b * nc + c, 0)),
        out_shape=jax.ShapeDtypeStruct((t, width), BF16),
        scratch_shapes=[pltpu.VMEM((DN_HEADS // 2, LANES, LANES), F32),
                        pltpu.VMEM((8, 3 * width), F32)],
        compiler_params=_params("arbitrary", "arbitrary"),
        name="dn",
    )(qkv, sm, z, conv_w, hp, ng, ones_blk, tri)


def _merge_body(yn_ref, yd_ref, mg_ref, x_ref, mod_ref, gf_ref, wb_ref, wo_ref, rwh_ref, rwl_ref,
                rb_ref, tri_ref, x1_ref, h_ref, rt_ref, cnt_ref):
    i = pl.program_id(0)
    d = x_ref.shape[1]

    @pl.when(i == 0)
    def _():
        cnt_ref[...] = jnp.zeros(cnt_ref.shape, F32)

    br0 = _dot(yn_ref[...], wb_ref[0])
    br1 = _dot(yd_ref[...], wb_ref[1])
    mixin = (_sigmoid(mg_ref[:, :d].astype(F32)) * br0 + _sigmoid(mg_ref[:, d:].astype(F32)) * br1)
    mix = _dot(mixin.astype(BF16), wo_ref[...])
    x1 = x_ref[...] + mod_ref[2:3, :] * mix
    x1_ref[...] = x1
    h = _rms_mod(x1, gf_ref[...], mod_ref[3:4, :], mod_ref[4:5, :])
    h_ref[...] = h

    hh = h.astype(BF16)
    hl = (h - hh.astype(F32)).astype(BF16)
    logits = _dot(hh, rwh_ref[...]) + _dot(hh, rwl_ref[...]) + _dot(hl, rwh_ref[...]) + rb_ref[...]
    lane = lax.broadcasted_iota(jnp.int32, (1, LANES), 1)
    cur = jnp.where(lane < N_EXPERTS, logits, PAD_SCORE)
    vals, idxs = [], []
    for _ in range(TOP_K):
        m = jnp.max(cur, axis=-1, keepdims=True)
        ix = jnp.min(jnp.where(cur == m, lane, LANES), axis=-1, keepdims=True)
        vals.append(m)
        idxs.append(ix)
        cur = jnp.where(lane == ix, PAD_SCORE, cur)
    es = [jnp.exp(v - vals[0]) for v in vals]
    den = es[0] + es[1] + es[2] + es[3]

    onehot = jnp.zeros(logits.shape, F32)
    for ix in idxs:
        onehot = onehot + jnp.where(lane == ix, 1.0, 0.0)
    before = _dot(tri_ref[...], onehot.astype(BF16)) + cnt_ref[...]
    cnt_ref[...] = cnt_ref[...] + jnp.sum(onehot, axis=0, keepdims=True)

    rt = jnp.zeros(logits.shape, F32)
    for k in range(TOP_K):
        rank = jnp.sum(jnp.where(lane == idxs[k], before, 0.0), axis=-1, keepdims=True)
        rt = jnp.where(lane == RT_IDX + k, idxs[k].astype(F32), rt)
        rt = jnp.where(lane == RT_W + k, es[k] / den, rt)
        rt = jnp.where(lane == RT_RANK + k, rank, rt)
    rt_ref[...] = rt


def _merge(y_nsa, y_dn, mg, x2, mod3, g_ffn, wb, wo, rwh, rwl, rb, seq):
    t, d = x2.shape
    tm = 256
    per_b = seq // tm
    hw = y_nsa.shape[1]
    tri = jnp.asarray(np.tril(np.ones((tm, tm)), -1), BF16)
    return pl.pallas_call(
        _merge_body,
        grid=(t // tm,),
        in_specs=[pl.BlockSpec((tm, hw), lambda i: (i, 0)),
                  pl.BlockSpec((tm, hw), lambda i: (i, 0)),
                  pl.BlockSpec((tm, 2 * d), lambda i: (i, 0)),
                  pl.BlockSpec((tm, d), lambda i: (i, 0)),
                  pl.BlockSpec((None, 6, d), lambda i: (i // per_b, 0, 0)),
                  pl.BlockSpec((1, d), lambda i: (0, 0)),
                  pl.BlockSpec(wb.shape, lambda i: (0, 0, 0)),
                  pl.BlockSpec(wo.shape, lambda i: (0, 0)),
                  pl.BlockSpec(rwh.shape, lambda i: (0, 0)),
                  pl.BlockSpec(rwl.shape, lambda i: (0, 0)),
                  pl.BlockSpec(rb.shape, lambda i: (0, 0)),
                  pl.BlockSpec(tri.shape, lambda i: (0, 0))],
        out_specs=[pl.BlockSpec((tm, d), lambda i: (i, 0)),
                   pl.BlockSpec((tm, d), lambda i: (i, 0)),
                   pl.BlockSpec((tm, LANES), lambda i: (i, 0)),
                   pl.BlockSpec((1, LANES), lambda i: (0, 0))],
        out_shape=[jax.ShapeDtypeStruct((t, d), F32),
                   jax.ShapeDtypeStruct((t, d), F32),
                   jax.ShapeDtypeStruct((t, LANES), F32),
                   jax.ShapeDtypeStruct((1, LANES), F32)],
        compiler_params=_params("arbitrary"),
        name="merge",
    )(y_nsa, y_dn, mg, x2, mod3, g_ffn.reshape(1, d), wb, wo, rwh, rwl, rb, tri)


def _row_copy(src, src_row, dst, dst_row, sem):
    return pltpu.make_async_copy(src.at[pl.ds(src_row, 1), :], dst.at[pl.ds(dst_row, 1), :], sem)


def _scatter_body(slot_ref, h_ref, xs_in_ref, xs_ref, sem):
    del xs_in_ref
    tm = h_ref.shape[0]

    def issue(r, carry):
        for k in range(TOP_K):
            _row_copy(h_ref, r, xs_ref, slot_ref[0, r * TOP_K + k], sem).start()
        return carry

    lax.fori_loop(0, tm, issue, 0)

    def drain(r, carry):
        for k in range(TOP_K):
            _row_copy(h_ref, r, xs_ref, slot_ref[0, r * TOP_K + k], sem).wait()
        return carry

    lax.fori_loop(0, tm, drain, 0)


def _scatter(slots3, h, xs_zero):
    t, d = h.shape
    tm = SCATTER_TILE
    return pl.pallas_call(
        _scatter_body,
        grid=(t // tm,),
        in_specs=[pl.BlockSpec((None, 1, tm * TOP_K), lambda i: (i, 0, 0), memory_space=pltpu.SMEM),
                  pl.BlockSpec((tm, d), lambda i: (i, 0)),
                  pl.BlockSpec(memory_space=pl.ANY)],
        out_specs=pl.BlockSpec(memory_space=pl.ANY),
        out_shape=jax.ShapeDtypeStruct(xs_zero.shape, F32),
        scratch_shapes=[pltpu.SemaphoreType.DMA(())],
        input_output_aliases={2: 0},
        compiler_params=_params("arbitrary"),
        name="scatter",
    )(slots3, h, xs_zero)


def _experts_body(te_ref, nv_ref, xs_ref, w1_ref, b1_ref, w2_ref, b2_ref, ys_ref):
    i = pl.program_id(0)
    f = w2_ref.shape[0]

    @pl.when(i < nv_ref[0])
    def _():
        xb = xs_ref[...].astype(BF16)
        u = _dot(xb, w1_ref[...]) + b1_ref[...]
        x_glu = jnp.minimum(u[:, :f], SWIGLU_LIMIT)
        x_lin = jnp.clip(u[:, f:], -SWIGLU_LIMIT, SWIGLU_LIMIT)
        act = x_glu * _sigmoid(SWIGLU_ALPHA * x_glu) * (x_lin + 1.0)
        ys_ref[...] = _dot(act.astype(BF16), w2_ref[...]) + b2_ref[...]

    @pl.when(i >= nv_ref[0])
    def _():
        ys_ref[...] = jnp.zeros(ys_ref.shape, F32)


def _experts(tile_expert, n_valid, xs, w1, b1, w2, b2):
    p, d = xs.shape
    tm = EXPERT_TILE
    f = w2.shape[1]
    grid_spec = pltpu.PrefetchScalarGridSpec(
        num_scalar_prefetch=2,
        grid=(p // tm,),
        in_specs=[pl.BlockSpec((tm, d), lambda i, te, nv: (i, 0)),
                  pl.BlockSpec((None, d, 2 * f), lambda i, te, nv: (te[i], 0, 0)),
                  pl.BlockSpec((None, 1, 2 * f), lambda i, te, nv: (te[i], 0, 0)),
                  pl.BlockSpec((None, f, d), lambda i, te, nv: (te[i], 0, 0)),
                  pl.BlockSpec((None, 1, d), lambda i, te, nv: (te[i], 0, 0))],
        out_specs=pl.BlockSpec((tm, d), lambda i, te, nv: (i, 0)),
    )
    return pl.pallas_call(
        _experts_body,
        grid_spec=grid_spec,
        out_shape=jax.ShapeDtypeStruct((p, d), F32),
        compiler_params=_params("arbitrary"),
        name="experts",
    )(tile_expert, n_valid, xs, w1, b1, w2, b2)


def _combine_body(slot_ref, rt_ref, x1_ref, mod_ref, g_ref, ys_ref, o_ref, buf_ref, sem):
    tm = x1_ref.shape[0]

    def issue(r, carry):
        for k in range(TOP_K):
            _row_copy(ys_ref, slot_ref[0, r * TOP_K + k], buf_ref.at[k], r, sem).start()
        return carry

    lax.fori_loop(0, tm, issue, 0)

    def drain(r, carry):
        for k in range(TOP_K):
            _row_copy(ys_ref, slot_ref[0, r * TOP_K + k], buf_ref.at[k], r, sem).wait()
        return carry

    lax.fori_loop(0, tm, drain, 0)

    rt = rt_ref[...]
    moe = rt[:, RT_W:RT_W + 1] * buf_ref[0]
    for k in range(1, TOP_K):
        moe = moe + rt[:, RT_W + k:RT_W + k + 1] * buf_ref[k]
    x2 = x1_ref[...] + mod_ref[5:6, :] * moe
    ms = jnp.mean(x2 * x2, axis=-1, keepdims=True)
    o_ref[...] = x2 * lax.rsqrt(ms + EPS) * g_ref[...]


def _combine(slots3, rt, x1, mod3, g_final, ys, seq):
    t, d = x1.shape
    tm = SCATTER_TILE
    per_b = seq // tm
    return pl.pallas_call(
        _combine_body,
        grid=(t // tm,),
        in_specs=[pl.BlockSpec((None, 1, tm * TOP_K), lambda i: (i, 0, 0), memory_space=pltpu.SMEM),
                  pl.BlockSpec((tm, LANES), lambda i: (i, 0)),
                  pl.BlockSpec((tm, d), lambda i: (i, 0)),
                  pl.BlockSpec((None, 6, d), lambda i: (i // per_b, 0, 0)),
                  pl.BlockSpec((1, d), lambda i: (0, 0)),
                  pl.BlockSpec(memory_space=pl.ANY)],
        out_specs=pl.BlockSpec((tm, d), lambda i: (i, 0)),
        out_shape=jax.ShapeDtypeStruct((t, d), F32),
        scratch_shapes=[pltpu.VMEM((TOP_K, tm, d), F32), pltpu.SemaphoreType.DMA(())],
        compiler_params=_params("arbitrary"),
        name="combine",
    )(slots3, rt, x1, mod3, g_final.reshape(1, d), ys)


def _pad_lanes(v, offset):
    out = jnp.zeros((1, LANES), F32)
    return out.at[0, offset:offset + v.shape[0]].set(v.astype(F32))


def kernel(x, c, w_ada, b_ada, g_norm_mix, w_in, cmp_pe_k, cmp_pe_v, cmp_w1, cmp_b1, cmp_w2,
           dn_conv_w, dn_a_log, dn_dt_bias, dn_norm_g, w_branch, w_out, g_norm_ffn,
           router_w, router_b, exp_w1, exp_b1, exp_w2, exp_b2, final_norm_g):
    bsz, seq, d = x.shape
    t = bsz * seq
    depth = w_ada.shape[0]
    assert depth == 1, "the final norm is fused into the last layer's combine step"
    x2 = x.reshape(t, d)
    cols = _in_columns()
    out = None
    for l in range(depth):
        mod3 = _ada(c, w_ada[l], b_ada[l]).reshape(bsz, 6, d)

        w_big = jnp.where(jnp.asarray(cols >= 0)[None, :], w_in[l][:, np.maximum(cols, 0)], 0.0).astype(BF16)
        q, kv, dnqkv, z, mg, sm = _inproj(x2, mod3, g_norm_mix[l], w_big, seq)

        nrow = seq // CMP_STRIDE
        src = kv.reshape(bsz, nrow, CMP_STRIDE, NSA_GROUPS, 6, HEAD_DIM)[:, :, :, :, 0:2, :]
        src = src.transpose(0, 3, 4, 1, 2, 5).reshape(bsz, NSA_GROUPS, 2, nrow, CMP_STRIDE * HEAD_DIM)
        pe = jnp.stack([cmp_pe_k[l], cmp_pe_v[l]]).reshape(2, 1, CMP_BLOCK * HEAD_DIM)
        pe = jnp.broadcast_to(pe, (2, 8, CMP_BLOCK * HEAD_DIM)).astype(BF16)
        w2p = jnp.zeros((2, CMP_HIDDEN, LANES), F32)
        w2p = w2p.at[0, :, :HEAD_DIM].set(cmp_w2[l, 0]).at[1, :, HEAD_DIM:].set(cmp_w2[l, 1]).astype(BF16)
        kcvc = _cmp(src, cmp_w1[l].astype(BF16), pe, cmp_b1[l].reshape(2, 1, CMP_HIDDEN), w2p)

        gates = sm[:, SM_GATE:SM_GATE + 3 * NSA_HEADS].reshape(bsz, seq, NSA_GROUPS, 3 * NSA_REP)
        gates = gates.transpose(0, 2, 1, 3)
        y_nsa = _nsa(q, kv, kcvc, gates, seq)

        hp = jnp.concatenate([_pad_lanes(dn_a_log[l], SM_A), _pad_lanes(dn_dt_bias[l], SM_A),
                              jnp.zeros((6, LANES), F32)], axis=0)
        ng = jnp.tile(dn_norm_g[l].reshape(1, HEAD_DIM), (1, 2))
        y_dn = _dn(dnqkv, sm, z, dn_conv_w[l], hp, ng, seq)

        rw = jnp.zeros((d, LANES), F32).at[:, :N_EXPERTS].set(router_w[l])
        rwh = rw.astype(BF16)
        rwl = (rw - rwh.astype(F32)).astype(BF16)
        x1, h, rt, cnt = _merge(y_nsa, y_dn, mg, x2, mod3, g_norm_ffn[l], w_branch[l].astype(BF16),
                                w_out[l].astype(BF16), rwh, rwl, _pad_lanes(router_b[l], 0), seq)

        counts = cnt[0, :N_EXPERTS].astype(jnp.int32)
        tiles_per = (counts + EXPERT_TILE - 1) // EXPERT_TILE
        tile_end = jnp.cumsum(tiles_per)
        offs = (tile_end - tiles_per) * EXPERT_TILE
        n_rows = t * TOP_K + N_EXPERTS * EXPERT_TILE
        n_tiles = n_rows // EXPERT_TILE
        idx = rt[:, RT_IDX:RT_IDX + TOP_K].astype(jnp.int32)
        rank = rt[:, RT_RANK:RT_RANK + TOP_K].astype(jnp.int32)
        slots = offs[idx] + rank
        slots3 = slots.reshape(t // SCATTER_TILE, 1, SCATTER_TILE * TOP_K)
        tile_expert = jnp.minimum(jnp.searchsorted(tile_end, jnp.arange(n_tiles, dtype=jnp.int32), side="right"),
                                  N_EXPERTS - 1).astype(jnp.int32)
        n_valid = tile_end[-1:].astype(jnp.int32)

        xs = _scatter(slots3, h, jnp.zeros((n_rows, d), F32))
        ys = _experts(tile_expert, n_valid, xs, exp_w1[l].astype(BF16),
                      exp_b1[l].reshape(N_EXPERTS, 1, -1), exp_w2[l].astype(BF16),
                      exp_b2[l].reshape(N_EXPERTS, 1, -1))
        out = _combine(slots3, rt, x1, mod3, final_norm_g, ys, seq)
    return out.reshape(bsz, seq, d)
```

```python
import functools
import math

import numpy as np
import jax
import jax.numpy as jnp
from jax import lax
from jax.experimental import pallas as pl
from jax.experimental.pallas import tpu as pltpu

F32 = jnp.float32
BF16 = jnp.bfloat16

HEAD_DIM = 64
NSA_HEADS = 8
NSA_GROUPS = 2
NSA_REP = NSA_HEADS // NSA_GROUPS
CMP_BLOCK = 32
CMP_STRIDE = 16
CMP_HIDDEN = 256
SEL_BLOCK = 64
SEL_TOP = 16
WINDOW = 512
Q_BLOCK = 128
DN_HEADS = 8
DN_CONV = 4
DN_CHUNK = 64
N_EXPERTS = 32
TOP_K = 4
SWIGLU_LIMIT = 7.0
SWIGLU_ALPHA = 1.702
EPS = 1e-6
MASK_VALUE = -1e30
FORCE_VALUE = 1e9
PAD_SCORE = -3e38

LANES = 128
VMEM_LIMIT = 56 * 2 ** 20

SEL_KC = 512
WIN_KEYS = WINDOW + Q_BLOCK
EXPERT_TILE = 512
SCATTER_TILE = 256

SM_GATE, SM_BETA, SM_A = 0, 24, 32
RT_IDX, RT_W, RT_RANK = 0, 4, 8


def _dot(a, b):
    return jnp.dot(a, b, preferred_element_type=F32)


def _dot_nt(a, b):
    return lax.dot_general(a, b, (((1,), (1,)), ((), ())), preferred_element_type=F32)


def _split3(x):
    hi = x.astype(BF16)
    r1 = x - hi.astype(F32)
    mid = r1.astype(BF16)
    lo = (r1 - mid.astype(F32)).astype(BF16)
    return hi, mid, lo


def _dot_f32_lhs(x, w_bf16):
    hi, mid, lo = _split3(x)
    return _dot(hi, w_bf16) + _dot(mid, w_bf16) + _dot(lo, w_bf16)


def _dot_f32_rhs(w_bf16, x):
    hi, mid, lo = _split3(x)
    return _dot(w_bf16, hi) + _dot(w_bf16, mid) + _dot(w_bf16, lo)


def _sigmoid(x):
    return 1.0 / (1.0 + jnp.exp(-x))


def _params(*sem):
    return pltpu.CompilerParams(dimension_semantics=sem, vmem_limit_bytes=VMEM_LIMIT)


def _ada_body(c_ref, w_ref, b_ref, o_ref):
    c = c_ref[...]
    a = (c * _sigmoid(c)).astype(BF16)
    o_ref[...] = _dot(a, w_ref[...].astype(BF16)) + b_ref[...]


def _ada(c, w, b):
    bsz, d = c.shape
    n = w.shape[1]
    tn = 1024
    return pl.pallas_call(
        _ada_body,
        grid=(n // tn,),
        in_specs=[pl.BlockSpec((bsz, d), lambda j: (0, 0)),
                  pl.BlockSpec((d, tn), lambda j: (0, j)),
                  pl.BlockSpec((1, tn), lambda j: (0, j))],
        out_specs=pl.BlockSpec((bsz, tn), lambda j: (0, j)),
        out_shape=jax.ShapeDtypeStruct((bsz, n), F32),
        compiler_params=_params("arbitrary"),
        name="ada",
    )(c, w, b.reshape(1, n))


IN_SEGS = (("q", 512, BF16), ("kv", 768, BF16), ("dn", 1536, BF16),
           ("z", 512, BF16), ("mg", 2048, BF16), ("sm", LANES, F32))


def _in_columns():
    q0 = 0
    kv0 = q0 + 512
    gate0 = kv0 + 768
    dn0 = gate0 + 24
    beta0 = dn0 + 1536
    a0 = beta0 + 8
    z0 = a0 + 8
    mg0 = z0 + 512
    cols = list(range(q0, q0 + 512))
    for g in range(NSA_GROUPS):
        for i in range(6):
            base = kv0 + i * NSA_GROUPS * HEAD_DIM + g * HEAD_DIM
            cols += list(range(base, base + HEAD_DIM))
    cols += list(range(dn0, dn0 + 1536))
    cols += list(range(z0, z0 + 512))
    cols += list(range(mg0, mg0 + 2048))
    small = list(range(gate0, gate0 + 24)) + list(range(beta0, beta0 + 8)) + list(range(a0, a0 + 8))
    cols += small + [-1] * (LANES - len(small))
    return np.asarray(cols, np.int32)


def _rms_mod(x, g, shift, scale):
    ms = jnp.mean(x * x, axis=-1, keepdims=True)
    y = x * lax.rsqrt(ms + EPS) * g
    return y * (1.0 + scale) + shift


def _inproj_body(x_ref, mod_ref, g_ref, w_ref, *out_refs):
    h = _rms_mod(x_ref[...], g_ref[...], mod_ref[0:1, :], mod_ref[1:2, :])
    hb = h.astype(BF16)
    off = 0
    for ref, (_, width, _) in zip(out_refs, IN_SEGS):
        for c0 in range(0, width, 512):
            cw = min(512, width - c0)
            ref[:, c0:c0 + cw] = _dot(hb, w_ref[:, off + c0:off + c0 + cw]).astype(ref.dtype)
        off += width


def _inproj(x2, mod3, g, w_big, seq):
    t, d = x2.shape
    tm = 512
    per_b = seq // tm
    nw = w_big.shape[1]
    return pl.pallas_call(
        _inproj_body,
        grid=(t // tm,),
        in_specs=[pl.BlockSpec((tm, d), lambda i: (i, 0)),
                  pl.BlockSpec((None, 6, d), lambda i: (i // per_b, 0, 0)),
                  pl.BlockSpec((1, d), lambda i: (0, 0)),
                  pl.BlockSpec((d, nw), lambda i: (0, 0))],
        out_specs=[pl.BlockSpec((tm, w), lambda i: (i, 0)) for _, w, _ in IN_SEGS],
        out_shape=[jax.ShapeDtypeStruct((t, w), dt) for _, w, dt in IN_SEGS],
        compiler_params=_params("arbitrary"),
        name="inproj",
    )(x2, mod3, g.reshape(1, d), w_big)


def _cmp_body(src_ref, w1_ref, pe_ref, b1_ref, w2_ref, w2t_ref, o_ref, ot_ref):
    half = CMP_STRIDE * HEAD_DIM
    out = None
    for kind in range(2):
        x = src_ref[kind]
        first = _dot(x, w1_ref[kind, :half, :])
        second = _dot(x, w1_ref[kind, half:, :])
        n = second.shape[0]
        second = pltpu.roll(second, n - 1, 0)
        pew = _dot(pe_ref[kind], w1_ref[kind])[0:1, :]
        pre = first + second + pew + b1_ref[kind]
        hid = (pre * _sigmoid(pre)).astype(BF16)
        term = _dot(hid, w2_ref[kind])
        out = term if out is None else out + term
    o_ref[...] = out.astype(o_ref.dtype)
    ot_ref[...] = _dot_nt(w2t_ref[...], hid).astype(ot_ref.dtype)


def _cmp(src, w1, pe, b1, w2p, w2t):
    bsz, ng, _, nrow, width = src.shape
    return pl.pallas_call(
        _cmp_body,
        grid=(bsz, ng),
        in_specs=[pl.BlockSpec((None, None, 2, nrow, width), lambda b, g: (b, g, 0, 0, 0)),
                  pl.BlockSpec(w1.shape, lambda b, g: (0, 0, 0)),
                  pl.BlockSpec(pe.shape, lambda b, g: (0, 0, 0)),
                  pl.BlockSpec(b1.shape, lambda b, g: (0, 0, 0)),
                  pl.BlockSpec(w2p.shape, lambda b, g: (0, 0, 0)),
                  pl.BlockSpec(w2t.shape, lambda b, g: (0, 0))],
        out_specs=[pl.BlockSpec((None, None, nrow, LANES), lambda b, g: (b, g, 0, 0)),
                   pl.BlockSpec((None, None, HEAD_DIM, nrow), lambda b, g: (b, g, 0, 0))],
        out_shape=[jax.ShapeDtypeStruct((bsz, ng, nrow, LANES), BF16),
                   jax.ShapeDtypeStruct((bsz, ng, HEAD_DIM, nrow), BF16)],
        compiler_params=_params("arbitrary", "arbitrary"),
        name="cmp",
    )(src, w1, pe, b1, w2p, w2t)


def _nsa_consts(seq):
    scale = HEAD_DIM ** -0.5
    gw = NSA_REP * HEAD_DIM
    pselt = np.zeros((NSA_REP, LANES, gw), np.float32)
    qout = np.zeros((NSA_REP, HEAD_DIM, gw), np.float32)
    for r in range(NSA_REP):
        for d in range(HEAD_DIM):
            pselt[r, d, r * HEAD_DIM + d] = scale
            qout[r, d, r * HEAD_DIM + d] = 1.0
    n_cmp = (seq - CMP_BLOCK) // CMP_STRIDE + 1
    n_sel = seq // SEL_BLOCK
    cs = np.arange(n_cmp)[:, None] * CMP_STRIDE
    ss = np.arange(n_sel)[None, :] * SEL_BLOCK
    ov = np.clip(np.minimum(cs + CMP_BLOCK, ss + SEL_BLOCK) - np.maximum(cs, ss), 0, None) / CMP_BLOCK
    overlap_t = np.zeros((n_sel, seq // CMP_STRIDE), np.float32)
    overlap_t[:, :n_cmp] = ov.T
    return jnp.asarray(pselt, BF16), jnp.asarray(qout, BF16), jnp.asarray(overlap_t, BF16)


def _masked_softmax0(s, allowed):
    s = jnp.where(allowed, s, MASK_VALUE)
    e = jnp.exp(s - jnp.max(s, axis=0, keepdims=True))
    p = e / jnp.sum(e, axis=0, keepdims=True)
    return jnp.where(allowed, p, 0.0)


def _nsa_body(q_ref, kv_ref, vst_ref, vwt_ref, kcvc_ref, vct_ref, gate_ref, pselt_ref, qout_ref,
              ovt_ref, o_ref, sb_ref, m_ref, l_ref, acc_ref, *, n_top, n_sel):
    qb = pl.program_id(2)
    rep, qn, hd = NSA_REP, Q_BLOCK, HEAD_DIM
    nq = rep * qn
    blocks_per_chunk = SEL_KC // SEL_BLOCK
    q2 = q_ref[...]
    qst = jnp.concatenate([_dot_nt(pselt_ref[r], q2) for r in range(rep)], axis=1).astype(BF16)
    tq = qb * qn + lax.broadcasted_iota(jnp.int32, (1, qn), 1)
    tq4 = jnp.concatenate([tq] * rep, axis=1)

    ncp = kcvc_ref.shape[0]
    s = _dot(kcvc_ref[...], qst)
    n_i = lax.broadcasted_iota(jnp.int32, (ncp, 1), 0)
    p = _masked_softmax0(s, (n_i * CMP_STRIDE + (CMP_BLOCK - 1)) <= tq4)
    o_cmp = _dot(vct_ref[...], p.astype(BF16))
    psum = p[:, 0:qn]
    for r in range(1, rep):
        psum = psum + p[:, r * qn:(r + 1) * qn]
    imp = _dot_f32_rhs(ovt_ref[...], psum)

    j = lax.broadcasted_iota(jnp.int32, (n_sel, 1), 0)
    cur = tq // SEL_BLOCK
    forced = (j == 0) | (j == cur) | (j == cur - 1)
    score = jnp.where(forced, FORCE_VALUE, jnp.where(j * SEL_BLOCK <= tq, imp, MASK_VALUE))
    cnt = jnp.zeros((n_sel, qn), F32)
    for jp in range(n_sel):
        row = score[jp:jp + 1, :]
        earlier = jnp.where(j > jp, 1.0, 0.0)
        cnt = cnt + jnp.where(row > score, 1.0, jnp.where(row == score, earlier, 0.0))
    selbias = jnp.where(cnt < n_top, 0.0, MASK_VALUE)
    selbias = jnp.concatenate([selbias] * rep, axis=1)
    for cc in range(n_sel // blocks_per_chunk):
        sb_ref[cc] = selbias[cc * blocks_per_chunk:(cc + 1) * blocks_per_chunk, :]

    m_ref[...] = jnp.full(m_ref.shape, MASK_VALUE, F32)
    l_ref[...] = jnp.zeros(l_ref.shape, F32)
    acc_ref[...] = jnp.zeros(acc_ref.shape, F32)

    def sel_chunk(c, diagonal):
        start = pl.multiple_of(c * SEL_KC, SEL_KC)
        sc = _dot(kv_ref[pl.ds(start, SEL_KC), LANES:2 * LANES], qst)
        bias = sb_ref[c]
        sc = jnp.concatenate([sc[b * SEL_BLOCK:(b + 1) * SEL_BLOCK, :] + bias[b:b + 1, :]
                              for b in range(blocks_per_chunk)], axis=0)
        if diagonal:
            kpos = start + lax.broadcasted_iota(jnp.int32, (SEL_KC, 1), 0)
            sc = jnp.where(kpos <= tq4, sc, MASK_VALUE)
        m_old = m_ref[...]
        m_new = jnp.maximum(m_old, jnp.max(sc, axis=0, keepdims=True))
        alpha = jnp.exp(m_old - m_new)
        pc = jnp.exp(sc - m_new)
        l_ref[...] = alpha * l_ref[...] + jnp.sum(pc, axis=0, keepdims=True)
        acc_ref[...] = alpha * acc_ref[...] + _dot(vst_ref[c], pc.astype(BF16))
        m_ref[...] = m_new

    last = (qb * qn) // SEL_KC

    def full_chunk(c, carry):
        sel_chunk(c, False)
        return carry

    lax.fori_loop(0, last, full_chunk, 0)
    sel_chunk(last, True)
    o_sel = acc_ref[...] / l_ref[...]

    c0 = jnp.maximum(qb - WINDOW // qn, 0)
    wstart = pl.multiple_of(c0 * qn, qn)
    sw = _dot(kv_ref[pl.ds(wstart, WIN_KEYS), 2 * LANES:3 * LANES], qst)
    delta = tq4 - (wstart + lax.broadcasted_iota(jnp.int32, (WIN_KEYS, 1), 0))
    pw = _masked_softmax0(sw, (delta >= 0) & (delta < WINDOW)).astype(BF16)
    o_win = _dot(vwt_ref[c0], pw[0:qn, :])
    for cc in range(1, WIN_KEYS // qn):
        o_win = o_win + _dot(vwt_ref[c0 + cc], pw[cc * qn:(cc + 1) * qn, :])

    gs = _sigmoid(gate_ref[...])
    out = None
    for r in range(rep):
        sl = slice(r * qn, (r + 1) * qn)
        o_r = (gs[3 * r:3 * r + 1, :] * o_cmp[:, sl] + gs[3 * r + 1:3 * r + 2, :] * o_sel[:, sl]
               + gs[3 * r + 2:3 * r + 3, :] * o_win[:, sl])
        term = _dot(o_r.T.astype(BF16), qout_ref[r])
        out = term if out is None else out + term
    o_ref[...] = out.astype(o_ref.dtype)


def _nsa(q, kv, vst, vwt, kcvc, vct, gates_t, seq):
    t = q.shape[0]
    bsz = t // seq
    nqb = seq // Q_BLOCK
    n_sel = seq // SEL_BLOCK
    n_top = min(SEL_TOP, n_sel)
    ncp = seq // CMP_STRIDE
    assert seq >= WIN_KEYS and seq % SEL_KC == 0 and SEL_KC % Q_BLOCK == 0 and n_sel % 8 == 0
    pselt, qout, overlap_t = _nsa_consts(seq)
    gw = NSA_REP * HEAD_DIM
    nq = NSA_REP * Q_BLOCK
    nck = seq // SEL_KC
    body = functools.partial(_nsa_body, n_top=n_top, n_sel=n_sel)
    return pl.pallas_call(
        body,
        grid=(bsz, NSA_GROUPS, nqb),
        in_specs=[pl.BlockSpec((Q_BLOCK, gw), lambda b, g, i: (b * nqb + i, g)),
                  pl.BlockSpec((seq, 3 * LANES), lambda b, g, i: (b, g)),
                  pl.BlockSpec((None, None, nck, HEAD_DIM, SEL_KC), lambda b, g, i: (b, g, 0, 0, 0)),
                  pl.BlockSpec((None, None, nqb, HEAD_DIM, Q_BLOCK), lambda b, g, i: (b, g, 0, 0, 0)),
                  pl.BlockSpec((None, None, ncp, LANES), lambda b, g, i: (b, g, 0, 0)),
                  pl.BlockSpec((None, None, HEAD_DIM, ncp), lambda b, g, i: (b, g, 0, 0)),
                  pl.BlockSpec((None, None, 3 * NSA_REP, Q_BLOCK), lambda b, g, i: (b, g, 0, i)),
                  pl.BlockSpec(pselt.shape, lambda b, g, i: (0, 0, 0)),
                  pl.BlockSpec(qout.shape, lambda b, g, i: (0, 0, 0)),
                  pl.BlockSpec(overlap_t.shape, lambda b, g, i: (0, 0))],
        out_specs=pl.BlockSpec((Q_BLOCK, gw), lambda b, g, i: (b * nqb + i, g)),
        out_shape=jax.ShapeDtypeStruct((t, NSA_HEADS * HEAD_DIM), BF16),
        scratch_shapes=[pltpu.VMEM((nck, SEL_KC // SEL_BLOCK, nq), F32),
                        pltpu.VMEM((1, nq), F32),
                        pltpu.VMEM((1, nq), F32),
                        pltpu.VMEM((HEAD_DIM, nq), F32)],
        compiler_params=_params("arbitrary", "arbitrary", "arbitrary"),
        name="nsa",
    )(q, kv, vst, vwt, kcvc, vct, gates_t, pselt, qout, overlap_t)


def _dn_body(qkv_ref, sm_ref, z_ref, cw_ref, hp_ref, ng_ref, ones_ref, tri_ref, o_ref,
             state_ref, prev_ref):
    c = pl.program_id(1)
    ch, hd = DN_CHUNK, HEAD_DIM
    width = DN_HEADS * hd
    nb = qkv_ref.shape[0]
    npair = DN_HEADS // 2
    units = [(b, p) for b in range(nb) for p in range(npair)]

    @pl.when(c == 0)
    def _():
        state_ref[...] = jnp.zeros(state_ref.shape, F32)
        prev_ref[...] = jnp.zeros(prev_ref.shape, F32)

    lane = lax.broadcasted_iota(jnp.int32, (1, LANES), 1)
    first = lane < hd
    ri = lax.broadcasted_iota(jnp.int32, (2 * ch, 1), 0)
    ci = lax.broadcasted_iota(jnp.int32, (1, 2 * ch), 1)
    same = (ri // ch) == (ci // ch)
    causal = same & ((ri % ch) >= (ci % ch))
    strict = same & ((ri % ch) > (ci % ch))
    blockdiag = (lax.broadcasted_iota(jnp.int32, (LANES, 1), 0) // hd) == (lane // hd)
    ones_blk = ones_ref[...]

    def stack(v):
        return jnp.concatenate([jnp.where(first, v, 0.0), jnp.where(first, 0.0, v)], axis=0)

    def fold(v):
        return v[:ch] + v[ch:]

    def head_sumsq(v):
        return _dot((v * v).astype(BF16), ones_blk)

    ys, betas, gcs = [], [], []
    for b in range(nb):
        x = qkv_ref[b].astype(F32)
        xe = jnp.concatenate([prev_ref[b], x], axis=0)
        y = cw_ref[DN_CONV - 1:DN_CONV, :] * x
        for tap in range(DN_CONV - 1):
            y = y + cw_ref[tap:tap + 1, :] * pltpu.roll(xe, DN_CONV - 1 - tap, 0)[8:, :]
        prev_ref[b] = x[ch - 8:, :]
        ys.append(y * _sigmoid(y))
        sm = sm_ref[b]
        betas.append(_sigmoid(sm))
        xa = sm + hp_ref[1:2, :]
        softplus = jnp.maximum(xa, 0.0) + jnp.log(1.0 + jnp.exp(-jnp.abs(xa)))
        gcs.append(_dot_f32_rhs(tri_ref[...], -jnp.exp(hp_ref[0:1, :]) * softplus))

    def bc(tile, base, p):
        return jnp.where(first, tile[:, base + 2 * p:base + 2 * p + 1], tile[:, base + 2 * p + 1:base + 2 * p + 2])

    qn, kn, vb, kb, gc, egc, dec = {}, {}, {}, {}, {}, {}, {}
    for u in units:
        b, p = u
        y = ys[b]
        qp = y[:, p * LANES:(p + 1) * LANES]
        kp = y[:, width + p * LANES:width + (p + 1) * LANES]
        vp = y[:, 2 * width + p * LANES:2 * width + (p + 1) * LANES]
        qn[u] = qp * lax.rsqrt(head_sumsq(qp) + EPS) * (hd ** -0.5)
        kn[u] = kp * lax.rsqrt(head_sumsq(kp) + EPS)
        beta = bc(betas[b], SM_BETA, p)
        gc[u] = bc(gcs[b], SM_A, p)
        egc[u] = jnp.exp(gc[u])
        kb[u] = kn[u] * beta
        vb[u] = vp * beta
        gcol = jnp.concatenate([gcs[b][:, SM_A + 2 * p:SM_A + 2 * p + 1],
                                gcs[b][:, SM_A + 2 * p + 1:SM_A + 2 * p + 2]], axis=0)
        gmat = jnp.broadcast_to(gcol, (2 * ch, 2 * ch))
        dec[u] = jnp.where(causal, jnp.exp(jnp.where(causal, gmat - gmat.T, 0.0)), 0.0)

    nmat, attn, sol = {}, {}, {}
    for u in units:
        ks = jnp.concatenate([kn[u], kn[u]], axis=0).astype(BF16)
        nmat[u] = -(_dot_nt(stack(kb[u]).astype(BF16), ks) * jnp.where(strict, dec[u], 0.0))
        attn[u] = (_dot_nt(stack(qn[u]).astype(BF16), ks) * dec[u]).astype(BF16)
        sol[u] = jnp.concatenate([stack(vb[u]), stack(kb[u] * egc[u])], axis=1)

    n_fac = int(math.log2(ch))
    for it in range(n_fac):
        for u in units:
            nb16 = nmat[u].astype(BF16)
            sol[u] = sol[u] + _dot(nb16, sol[u].astype(BF16))
            if it < n_fac - 1:
                nmat[u] = _dot(nb16, nb16)

    v_new, st = {}, {}
    for u in units:
        b, p = u
        st[u] = state_ref[b, p]
        v_new[u] = fold(sol[u][:, :LANES]) - _dot(fold(sol[u][:, LANES:]).astype(BF16), st[u].astype(BF16))

    o = {}
    for u in units:
        b, p = u
        glast = gc[u][ch - 1:ch, :]
        o[u] = (_dot((qn[u] * egc[u]).astype(BF16), st[u].astype(BF16))
                + fold(_dot(attn[u], stack(v_new[u]).astype(BF16))))
        k_dec = kn[u] * jnp.exp(glast - gc[u])
        upd = _dot(k_dec.T.astype(BF16), v_new[u].astype(BF16))
        state_ref[b, p] = st[u] * jnp.exp(glast) + jnp.where(blockdiag, upd, 0.0)

    for u in units:
        b, p = u
        sl = slice(p * LANES, (p + 1) * LANES)
        on = o[u] * lax.rsqrt(head_sumsq(o[u]) * (1.0 / hd) + EPS) * ng_ref[...]
        zp = z_ref[b, :, sl].astype(F32)
        o_ref[b, :, sl] = (on * (zp * _sigmoid(zp))).astype(o_ref.dtype)


def _dn(qkv, sm, z, conv_w, hp, ng, seq):
    t = qkv.shape[0]
    bsz = t // seq
    nb = 2 if bsz % 2 == 0 else 1
    nc = seq // DN_CHUNK
    width = DN_HEADS * HEAD_DIM
    ones_blk = jnp.asarray(np.kron(np.eye(2), np.ones((HEAD_DIM, HEAD_DIM))), BF16)
    tri = jnp.asarray(np.tril(np.ones((DN_CHUNK, DN_CHUNK))), BF16)
    y = pl.pallas_call(
        _dn_body,
        grid=(bsz // nb, nc),
        in_specs=[pl.BlockSpec((nb, DN_CHUNK, 3 * width), lambda b, c: (b, c, 0)),
                  pl.BlockSpec((nb, DN_CHUNK, LANES), lambda b, c: (b, c, 0)),
                  pl.BlockSpec((nb, DN_CHUNK, width), lambda b, c: (b, c, 0)),
                  pl.BlockSpec(conv_w.shape, lambda b, c: (0, 0)),
                  pl.BlockSpec(hp.shape, lambda b, c: (0, 0)),
                  pl.BlockSpec(ng.shape, lambda b, c: (0, 0)),
                  pl.BlockSpec(ones_blk.shape, lambda b, c: (0, 0)),
                  pl.BlockSpec(tri.shape, lambda b, c: (0, 0))],
        out_specs=pl.BlockSpec((nb, DN_CHUNK, width), lambda b, c: (b, c, 0)),
        out_shape=jax.ShapeDtypeStruct((bsz, seq, width), BF16),
        scratch_shapes=[pltpu.VMEM((nb, DN_HEADS // 2, LANES, LANES), F32),
                        pltpu.VMEM((nb, 8, 3 * width), F32)],
        compiler_params=_params("arbitrary", "arbitrary"),
        name="dn",
    )(qkv.reshape(bsz, seq, 3 * width), sm.reshape(bsz, seq, LANES), z.reshape(bsz, seq, width),
      conv_w, hp, ng, ones_blk, tri)
    return y.reshape(t, width)


def _merge_body(yn_ref, yd_ref, mg_ref, x_ref, mod_ref, gf_ref, wb_ref, wo_ref, rwh_ref, rwl_ref,
                rb_ref, tri_ref, x1_ref, h_ref, rt_ref, cnt_ref):
    i = pl.program_id(0)
    d = x_ref.shape[1]

    @pl.when(i == 0)
    def _():
        cnt_ref[...] = jnp.zeros(cnt_ref.shape, F32)

    br0 = _dot(yn_ref[...], wb_ref[0])
    br1 = _dot(yd_ref[...], wb_ref[1])
    mixin = (_sigmoid(mg_ref[:, :d].astype(F32)) * br0 + _sigmoid(mg_ref[:, d:].astype(F32)) * br1)
    mix = _dot(mixin.astype(BF16), wo_ref[...])
    x1 = x_ref[...] + mod_ref[2:3, :] * mix
    x1_ref[...] = x1
    h = _rms_mod(x1, gf_ref[...], mod_ref[3:4, :], mod_ref[4:5, :])
    h_ref[...] = h

    hh = h.astype(BF16)
    hl = (h - hh.astype(F32)).astype(BF16)
    logits = _dot(hh, rwh_ref[...]) + _dot(hh, rwl_ref[...]) + _dot(hl, rwh_ref[...]) + rb_ref[...]
    lane = lax.broadcasted_iota(jnp.int32, (1, LANES), 1)
    cur = jnp.where(lane < N_EXPERTS, logits, PAD_SCORE)
    vals, idxs = [], []
    for _ in range(TOP_K):
        m = jnp.max(cur, axis=-1, keepdims=True)
        ix = jnp.min(jnp.where(cur == m, lane, LANES), axis=-1, keepdims=True)
        vals.append(m)
        idxs.append(ix)
        cur = jnp.where(lane == ix, PAD_SCORE, cur)
    es = [jnp.exp(v - vals[0]) for v in vals]
    den = es[0] + es[1] + es[2] + es[3]

    onehot = jnp.zeros(logits.shape, F32)
    for ix in idxs:
        onehot = onehot + jnp.where(lane == ix, 1.0, 0.0)
    before = _dot(tri_ref[...], onehot.astype(BF16)) + cnt_ref[...]
    cnt_ref[...] = cnt_ref[...] + jnp.sum(onehot, axis=0, keepdims=True)

    rt = jnp.zeros(logits.shape, F32)
    for k in range(TOP_K):
        rank = jnp.sum(jnp.where(lane == idxs[k], before, 0.0), axis=-1, keepdims=True)
        rt = jnp.where(lane == RT_IDX + k, idxs[k].astype(F32), rt)
        rt = jnp.where(lane == RT_W + k, es[k] / den, rt)
        rt = jnp.where(lane == RT_RANK + k, rank, rt)
    rt_ref[...] = rt


def _merge(y_nsa, y_dn, mg, x2, mod3, g_ffn, wb, wo, rwh, rwl, rb, seq):
    t, d = x2.shape
    tm = 256
    per_b = seq // tm
    hw = y_nsa.shape[1]
    tri = jnp.asarray(np.tril(np.ones((tm, tm)), -1), BF16)
    return pl.pallas_call(
        _merge_body,
        grid=(t // tm,),
        in_specs=[pl.BlockSpec((tm, hw), lambda i: (i, 0)),
                  pl.BlockSpec((tm, hw), lambda i: (i, 0)),
                  pl.BlockSpec((tm, 2 * d), lambda i: (i, 0)),
                  pl.BlockSpec((tm, d), lambda i: (i, 0)),
                  pl.BlockSpec((None, 6, d), lambda i: (i // per_b, 0, 0)),
                  pl.BlockSpec((1, d), lambda i: (0, 0)),
                  pl.BlockSpec(wb.shape, lambda i: (0, 0, 0)),
                  pl.BlockSpec(wo.shape, lambda i: (0, 0)),
                  pl.BlockSpec(rwh.shape, lambda i: (0, 0)),
                  pl.BlockSpec(rwl.shape, lambda i: (0, 0)),
                  pl.BlockSpec(rb.shape, lambda i: (0, 0)),
                  pl.BlockSpec(tri.shape, lambda i: (0, 0))],
        out_specs=[pl.BlockSpec((tm, d), lambda i: (i, 0)),
                   pl.BlockSpec((tm, d), lambda i: (i, 0)),
                   pl.BlockSpec((tm, LANES), lambda i: (i, 0)),
                   pl.BlockSpec((1, LANES), lambda i: (0, 0))],
        out_shape=[jax.ShapeDtypeStruct((t, d), F32),
                   jax.ShapeDtypeStruct((t, d), F32),
                   jax.ShapeDtypeStruct((t, LANES), F32),
                   jax.ShapeDtypeStruct((1, LANES), F32)],
        compiler_params=_params("arbitrary"),
        name="merge",
    )(y_nsa, y_dn, mg, x2, mod3, g_ffn.reshape(1, d), wb, wo, rwh, rwl, rb, tri)


def _row_copy(src, src_row, dst, dst_row, sem):
    return pltpu.make_async_copy(src.at[pl.ds(src_row, 1), :], dst.at[pl.ds(dst_row, 1), :], sem)


def _scatter_body(slot_ref, h_ref, xs_in_ref, xs_ref, sem):
    del xs_in_ref
    tm = h_ref.shape[0]

    def issue(r, carry):
        for k in range(TOP_K):
            _row_copy(h_ref, r, xs_ref, slot_ref[0, r * TOP_K + k], sem).start()
        return carry

    lax.fori_loop(0, tm, issue, 0)

    def drain(r, carry):
        for k in range(TOP_K):
            _row_copy(h_ref, r, xs_ref, slot_ref[0, r * TOP_K + k], sem).wait()
        return carry

    lax.fori_loop(0, tm, drain, 0)


def _scatter(slots3, h, xs_zero):
    t, d = h.shape
    tm = SCATTER_TILE
    return pl.pallas_call(
        _scatter_body,
        grid=(t // tm,),
        in_specs=[pl.BlockSpec((None, 1, tm * TOP_K), lambda i: (i, 0, 0), memory_space=pltpu.SMEM),
                  pl.BlockSpec((tm, d), lambda i: (i, 0)),
                  pl.BlockSpec(memory_space=pl.ANY)],
        out_specs=pl.BlockSpec(memory_space=pl.ANY),
        out_shape=jax.ShapeDtypeStruct(xs_zero.shape, F32),
        scratch_shapes=[pltpu.SemaphoreType.DMA(())],
        input_output_aliases={2: 0},
        compiler_params=_params("arbitrary"),
        name="scatter",
    )(slots3, h, xs_zero)


def _experts_body(te_ref, nv_ref, xs_ref, w1_ref, b1_ref, w2_ref, b2_ref, ys_ref):
    i = pl.program_id(0)
    f = w2_ref.shape[0]

    @pl.when(i < nv_ref[0])
    def _():
        xb = xs_ref[...].astype(BF16)
        u = _dot(xb, w1_ref[...]) + b1_ref[...]
        x_glu = jnp.minimum(u[:, :f], SWIGLU_LIMIT)
        x_lin = jnp.clip(u[:, f:], -SWIGLU_LIMIT, SWIGLU_LIMIT)
        act = x_glu * _sigmoid(SWIGLU_ALPHA * x_glu) * (x_lin + 1.0)
        ys_ref[...] = _dot(act.astype(BF16), w2_ref[...]) + b2_ref[...]

    @pl.when(i >= nv_ref[0])
    def _():
        ys_ref[...] = jnp.zeros(ys_ref.shape, F32)


def _experts(tile_expert, n_valid, xs, w1, b1, w2, b2):
    p, d = xs.shape
    tm = EXPERT_TILE
    f = w2.shape[1]
    grid_spec = pltpu.PrefetchScalarGridSpec(
        num_scalar_prefetch=2,
        grid=(p // tm,),
        in_specs=[pl.BlockSpec((tm, d), lambda i, te, nv: (i, 0)),
                  pl.BlockSpec((None, d, 2 * f), lambda i, te, nv: (te[i], 0, 0)),
                  pl.BlockSpec((None, 1, 2 * f), lambda i, te, nv: (te[i], 0, 0)),
                  pl.BlockSpec((None, f, d), lambda i, te, nv: (te[i], 0, 0)),
                  pl.BlockSpec((None, 1, d), lambda i, te, nv: (te[i], 0, 0))],
        out_specs=pl.BlockSpec((tm, d), lambda i, te, nv: (i, 0)),
    )
    return pl.pallas_call(
        _experts_body,
        grid_spec=grid_spec,
        out_shape=jax.ShapeDtypeStruct((p, d), F32),
        compiler_params=_params("arbitrary"),
        name="experts",
    )(tile_expert, n_valid, xs, w1, b1, w2, b2)


def _combine_body(slot_ref, rt_ref, x1_ref, mod_ref, g_ref, ys_ref, o_ref, buf_ref, sem):
    tm = x1_ref.shape[0]

    def issue(r, carry):
        for k in range(TOP_K):
            _row_copy(ys_ref, slot_ref[0, r * TOP_K + k], buf_ref.at[k], r, sem).start()
        return carry

    lax.fori_loop(0, tm, issue, 0)

    def drain(r, carry):
        for k in range(TOP_K):
            _row_copy(ys_ref, slot_ref[0, r * TOP_K + k], buf_ref.at[k], r, sem).wait()
        return carry

    lax.fori_loop(0, tm, drain, 0)

    rt = rt_ref[...]
    moe = rt[:, RT_W:RT_W + 1] * buf_ref[0]
    for k in range(1, TOP_K):
        moe = moe + rt[:, RT_W + k:RT_W + k + 1] * buf_ref[k]
    x2 = x1_ref[...] + mod_ref[5:6, :] * moe
    ms = jnp.mean(x2 * x2, axis=-1, keepdims=True)
    o_ref[...] = x2 * lax.rsqrt(ms + EPS) * g_ref[...]


def _combine(slots3, rt, x1, mod3, g_final, ys, seq):
    t, d = x1.shape
    tm = SCATTER_TILE
    per_b = seq // tm
    return pl.pallas_call(
        _combine_body,
        grid=(t // tm,),
        in_specs=[pl.BlockSpec((None, 1, tm * TOP_K), lambda i: (i, 0, 0), memory_space=pltpu.SMEM),
                  pl.BlockSpec((tm, LANES), lambda i: (i, 0)),
                  pl.BlockSpec((tm, d), lambda i: (i, 0)),
                  pl.BlockSpec((None, 6, d), lambda i: (i // per_b, 0, 0)),
                  pl.BlockSpec((1, d), lambda i: (0, 0)),
                  pl.BlockSpec(memory_space=pl.ANY)],
        out_specs=pl.BlockSpec((tm, d), lambda i: (i, 0)),
        out_shape=jax.ShapeDtypeStruct((t, d), F32),
        scratch_shapes=[pltpu.VMEM((TOP_K, tm, d), F32), pltpu.SemaphoreType.DMA(())],
        compiler_params=_params("arbitrary"),
        name="combine",
    )(slots3, rt, x1, mod3, g_final.reshape(1, d), ys)


def _pad_lanes(v, offset):
    out = jnp.zeros((1, LANES), F32)
    return out.at[0, offset:offset + v.shape[0]].set(v.astype(F32))


def kernel(x, c, w_ada, b_ada, g_norm_mix, w_in, cmp_pe_k, cmp_pe_v, cmp_w1, cmp_b1, cmp_w2,
           dn_conv_w, dn_a_log, dn_dt_bias, dn_norm_g, w_branch, w_out, g_norm_ffn,
           router_w, router_b, exp_w1, exp_b1, exp_w2, exp_b2, final_norm_g):
    bsz, seq, d = x.shape
    t = bsz * seq
    depth = w_ada.shape[0]
    assert depth == 1, "the final norm is fused into the last layer's combine step"
    x2 = x.reshape(t, d)
    cols = _in_columns()
    out = None
    for l in range(depth):
        mod3 = _ada(c, w_ada[l], b_ada[l]).reshape(bsz, 6, d)

        w_big = jnp.where(jnp.asarray(cols >= 0)[None, :], w_in[l][:, np.maximum(cols, 0)], 0.0).astype(BF16)
        q, kv, dnqkv, z, mg, sm = _inproj(x2, mod3, g_norm_mix[l], w_big, seq)

        nrow = seq // CMP_STRIDE
        src = kv.reshape(bsz, nrow, CMP_STRIDE, NSA_GROUPS, 6, HEAD_DIM)[:, :, :, :, 0:2, :]
        src = src.transpose(0, 3, 4, 1, 2, 5).reshape(bsz, NSA_GROUPS, 2, nrow, CMP_STRIDE * HEAD_DIM)
        pe = jnp.stack([cmp_pe_k[l], cmp_pe_v[l]]).reshape(2, 1, CMP_BLOCK * HEAD_DIM)
        pe = jnp.broadcast_to(pe, (2, 8, CMP_BLOCK * HEAD_DIM)).astype(BF16)
        w2p = jnp.zeros((2, CMP_HIDDEN, LANES), F32)
        w2p = w2p.at[0, :, :HEAD_DIM].set(cmp_w2[l, 0]).at[1, :, HEAD_DIM:].set(cmp_w2[l, 1]).astype(BF16)
        kcvc, vct = _cmp(src, cmp_w1[l].astype(BF16), pe, cmp_b1[l].reshape(2, 1, CMP_HIDDEN), w2p,
                         cmp_w2[l, 1].T.astype(BF16))

        kv5 = kv.reshape(bsz, seq, NSA_GROUPS, 3, LANES)
        vst = kv5[:, :, :, 1, HEAD_DIM:].reshape(bsz, seq // SEL_KC, SEL_KC, NSA_GROUPS, HEAD_DIM)
        vst = vst.transpose(0, 3, 1, 4, 2)
        vwt = kv5[:, :, :, 2, HEAD_DIM:].reshape(bsz, seq // Q_BLOCK, Q_BLOCK, NSA_GROUPS, HEAD_DIM)
        vwt = vwt.transpose(0, 3, 1, 4, 2)
        gates_t = sm[:, SM_GATE:SM_GATE + 3 * NSA_HEADS].reshape(bsz, seq, NSA_GROUPS, 3 * NSA_REP)
        gates_t = gates_t.transpose(0, 2, 3, 1)
        y_nsa = _nsa(q, kv, vst, vwt, kcvc, vct, gates_t, seq)

        hp = jnp.concatenate([_pad_lanes(dn_a_log[l], SM_A), _pad_lanes(dn_dt_bias[l], SM_A),
                              jnp.zeros((6, LANES), F32)], axis=0)
        ng = jnp.tile(dn_norm_g[l].reshape(1, HEAD_DIM), (1, 2))
        y_dn = _dn(dnqkv, sm, z, dn_conv_w[l], hp, ng, seq)

        rw = jnp.zeros((d, LANES), F32).at[:, :N_EXPERTS].set(router_w[l])
        rwh = rw.astype(BF16)
        rwl = (rw - rwh.astype(F32)).astype(BF16)
        x1, h, rt, cnt = _merge(y_nsa, y_dn, mg, x2, mod3, g_norm_ffn[l], w_branch[l].astype(BF16),
                                w_out[l].astype(BF16), rwh, rwl, _pad_lanes(router_b[l], 0), seq)

        counts = cnt[0, :N_EXPERTS].astype(jnp.int32)
        tiles_per = (counts + EXPERT_TILE - 1) // EXPERT_TILE
        tile_end = jnp.cumsum(tiles_per)
        offs = (tile_end - tiles_per) * EXPERT_TILE
        n_rows = t * TOP_K + N_EXPERTS * EXPERT_TILE
        n_tiles = n_rows // EXPERT_TILE
        idx = rt[:, RT_IDX:RT_IDX + TOP_K].astype(jnp.int32)
        rank = rt[:, RT_RANK:RT_RANK + TOP_K].astype(jnp.int32)
        slots = offs[idx] + rank
        slots3 = slots.reshape(t // SCATTER_TILE, 1, SCATTER_TILE * TOP_K)
        tile_ids = jnp.arange(n_tiles, dtype=jnp.int32)
        tile_expert = jnp.minimum(jnp.sum((tile_ids[:, None] >= tile_end[None, :]).astype(jnp.int32), axis=1),
                                  N_EXPERTS - 1).astype(jnp.int32)
        n_valid = tile_end[-1:].astype(jnp.int32)

        xs = _scatter(slots3, h, jnp.zeros((n_rows, d), F32))
        ys = _experts(tile_expert, n_valid, xs, exp_w1[l].astype(BF16),
                      exp_b1[l].reshape(N_EXPERTS, 1, -1), exp_w2[l].astype(BF16),
                      exp_b2[l].reshape(N_EXPERTS, 1, -1))
        out = _combine(slots3, rt, x1, mod3, final_norm_g, ys, seq)
    return out.reshape(bsz, seq, d)
```

```python
import functools
import math

import numpy as np
import jax
import jax.numpy as jnp
from jax import lax
from jax.experimental import pallas as pl
from jax.experimental.pallas import tpu as pltpu

F32 = jnp.float32
BF16 = jnp.bfloat16

HEAD_DIM = 64
NSA_HEADS = 8
NSA_GROUPS = 2
NSA_REP = NSA_HEADS // NSA_GROUPS
CMP_BLOCK = 32
CMP_STRIDE = 16
CMP_HIDDEN = 256
SEL_BLOCK = 64
SEL_TOP = 16
WINDOW = 512
Q_BLOCK = 128
DN_HEADS = 8
DN_CONV = 4
DN_CHUNK = 64
N_EXPERTS = 32
TOP_K = 4
SWIGLU_LIMIT = 7.0
SWIGLU_ALPHA = 1.702
EPS = 1e-6
MASK_VALUE = -1e30
FORCE_VALUE = 1e9
PAD_SCORE = -3e38

LANES = 128
VMEM_LIMIT = 56 * 2 ** 20

SEL_KC = 512
WIN_KEYS = WINDOW + Q_BLOCK
EXPERT_TILE = 512
SCATTER_TILE = 256

SM_GATE, SM_BETA, SM_A = 0, 24, 32
RT_IDX, RT_W, RT_RANK = 0, 4, 8


def _dot(a, b):
    return jnp.dot(a, b, preferred_element_type=F32)


def _dot_nt(a, b):
    return lax.dot_general(a, b, (((1,), (1,)), ((), ())), preferred_element_type=F32)


def _split3(x):
    hi = x.astype(BF16)
    r1 = x - hi.astype(F32)
    mid = r1.astype(BF16)
    lo = (r1 - mid.astype(F32)).astype(BF16)
    return hi, mid, lo


def _dot_f32_lhs(x, w_bf16):
    hi, mid, lo = _split3(x)
    return _dot(hi, w_bf16) + _dot(mid, w_bf16) + _dot(lo, w_bf16)


def _dot_f32_rhs(w_bf16, x):
    hi, mid, lo = _split3(x)
    return _dot(w_bf16, hi) + _dot(w_bf16, mid) + _dot(w_bf16, lo)


def _sigmoid(x):
    return 1.0 / (1.0 + jnp.exp(-x))


def _params(*sem):
    return pltpu.CompilerParams(dimension_semantics=sem, vmem_limit_bytes=VMEM_LIMIT)


def _ada_body(c_ref, w_ref, b_ref, o_ref):
    c = c_ref[...]
    a = (c * _sigmoid(c)).astype(BF16)
    o_ref[...] = _dot(a, w_ref[...].astype(BF16)) + b_ref[...]


def _ada(c, w, b):
    bsz, d = c.shape
    n = w.shape[1]
    tn = 1024
    return pl.pallas_call(
        _ada_body,
        grid=(n // tn,),
        in_specs=[pl.BlockSpec((bsz, d), lambda j: (0, 0)),
                  pl.BlockSpec((d, tn), lambda j: (0, j)),
                  pl.BlockSpec((1, tn), lambda j: (0, j))],
        out_specs=pl.BlockSpec((bsz, tn), lambda j: (0, j)),
        out_shape=jax.ShapeDtypeStruct((bsz, n), F32),
        compiler_params=_params("arbitrary"),
        name="ada",
    )(c, w, b.reshape(1, n))


IN_SEGS = (("q", 512, BF16), ("kv", 768, BF16), ("dn", 1536, BF16),
           ("z", 512, BF16), ("mg", 2048, BF16), ("sm", LANES, F32))


def _in_columns():
    q0 = 0
    kv0 = q0 + 512
    gate0 = kv0 + 768
    dn0 = gate0 + 24
    beta0 = dn0 + 1536
    a0 = beta0 + 8
    z0 = a0 + 8
    mg0 = z0 + 512
    cols = list(range(q0, q0 + 512))
    for g in range(NSA_GROUPS):
        for i in range(6):
            base = kv0 + i * NSA_GROUPS * HEAD_DIM + g * HEAD_DIM
            cols += list(range(base, base + HEAD_DIM))
    cols += list(range(dn0, dn0 + 1536))
    cols += list(range(z0, z0 + 512))
    cols += list(range(mg0, mg0 + 2048))
    small = list(range(gate0, gate0 + 24)) + list(range(beta0, beta0 + 8)) + list(range(a0, a0 + 8))
    cols += small + [-1] * (LANES - len(small))
    return np.asarray(cols, np.int32)


def _rms_mod(x, g, shift, scale):
    ms = jnp.mean(x * x, axis=-1, keepdims=True)
    y = x * lax.rsqrt(ms + EPS) * g
    return y * (1.0 + scale) + shift


def _inproj_body(x_ref, mod_ref, g_ref, w_ref, *out_refs):
    h = _rms_mod(x_ref[...], g_ref[...], mod_ref[0:1, :], mod_ref[1:2, :])
    hb = h.astype(BF16)
    off = 0
    for ref, (_, width, _) in zip(out_refs, IN_SEGS):
        for c0 in range(0, width, 512):
            cw = min(512, width - c0)
            ref[:, c0:c0 + cw] = _dot(hb, w_ref[:, off + c0:off + c0 + cw]).astype(ref.dtype)
        off += width


def _inproj(x2, mod3, g, w_big, seq):
    t, d = x2.shape
    tm = 512
    per_b = seq // tm
    nw = w_big.shape[1]
    return pl.pallas_call(
        _inproj_body,
        grid=(t // tm,),
        in_specs=[pl.BlockSpec((tm, d), lambda i: (i, 0)),
                  pl.BlockSpec((None, 6, d), lambda i: (i // per_b, 0, 0)),
                  pl.BlockSpec((1, d), lambda i: (0, 0)),
                  pl.BlockSpec((d, nw), lambda i: (0, 0))],
        out_specs=[pl.BlockSpec((tm, w), lambda i: (i, 0)) for _, w, _ in IN_SEGS],
        out_shape=[jax.ShapeDtypeStruct((t, w), dt) for _, w, dt in IN_SEGS],
        compiler_params=_params("arbitrary"),
        name="inproj",
    )(x2, mod3, g.reshape(1, d), w_big)


def _cmp_body(src_ref, w1_ref, pe_ref, b1_ref, w2_ref, w2t_ref, o_ref, ot_ref):
    half = CMP_STRIDE * HEAD_DIM
    out = None
    for kind in range(2):
        x = src_ref[kind]
        first = _dot(x, w1_ref[kind, :half, :])
        second = _dot(x, w1_ref[kind, half:, :])
        n = second.shape[0]
        second = pltpu.roll(second, n - 1, 0)
        pew = _dot(pe_ref[kind], w1_ref[kind])[0:1, :]
        pre = first + second + pew + b1_ref[kind]
        hid = (pre * _sigmoid(pre)).astype(BF16)
        term = _dot(hid, w2_ref[kind])
        out = term if out is None else out + term
    o_ref[...] = out.astype(o_ref.dtype)
    ot_ref[...] = _dot_nt(w2t_ref[...], hid).astype(ot_ref.dtype)


def _cmp(src, w1, pe, b1, w2p, w2t):
    bsz, ng, _, nrow, width = src.shape
    return pl.pallas_call(
        _cmp_body,
        grid=(bsz, ng),
        in_specs=[pl.BlockSpec((None, None, 2, nrow, width), lambda b, g: (b, g, 0, 0, 0)),
                  pl.BlockSpec(w1.shape, lambda b, g: (0, 0, 0)),
                  pl.BlockSpec(pe.shape, lambda b, g: (0, 0, 0)),
                  pl.BlockSpec(b1.shape, lambda b, g: (0, 0, 0)),
                  pl.BlockSpec(w2p.shape, lambda b, g: (0, 0, 0)),
                  pl.BlockSpec(w2t.shape, lambda b, g: (0, 0))],
        out_specs=[pl.BlockSpec((None, None, nrow, LANES), lambda b, g: (b, g, 0, 0)),
                   pl.BlockSpec((None, None, HEAD_DIM, nrow), lambda b, g: (b, g, 0, 0))],
        out_shape=[jax.ShapeDtypeStruct((bsz, ng, nrow, LANES), BF16),
                   jax.ShapeDtypeStruct((bsz, ng, HEAD_DIM, nrow), BF16)],
        compiler_params=_params("arbitrary", "arbitrary"),
        name="cmp",
    )(src, w1, pe, b1, w2p, w2t)


def _nsa_consts(seq):
    scale = HEAD_DIM ** -0.5
    gw = NSA_REP * HEAD_DIM
    pselt = np.zeros((NSA_REP, LANES, gw), np.float32)
    qout = np.zeros((NSA_REP, HEAD_DIM, gw), np.float32)
    for r in range(NSA_REP):
        for d in range(HEAD_DIM):
            pselt[r, d, r * HEAD_DIM + d] = scale
            qout[r, d, r * HEAD_DIM + d] = 1.0
    n_cmp = (seq - CMP_BLOCK) // CMP_STRIDE + 1
    n_sel = seq // SEL_BLOCK
    cs = np.arange(n_cmp)[:, None] * CMP_STRIDE
    ss = np.arange(n_sel)[None, :] * SEL_BLOCK
    ov = np.clip(np.minimum(cs + CMP_BLOCK, ss + SEL_BLOCK) - np.maximum(cs, ss), 0, None) / CMP_BLOCK
    overlap_t = np.zeros((n_sel, seq // CMP_STRIDE), np.float32)
    overlap_t[:, :n_cmp] = ov.T
    qrel = np.tile(np.arange(Q_BLOCK), NSA_REP)[None, None, :]
    shift = (np.arange(WINDOW // Q_BLOCK + 1) * Q_BLOCK)[:, None, None]
    delta = shift + qrel - np.arange(WIN_KEYS)[None, :, None]
    win_bias = np.where((delta >= 0) & (delta < WINDOW), 0.0, MASK_VALUE).astype(np.float32)
    shift = (np.arange(SEL_KC // Q_BLOCK) * Q_BLOCK)[:, None, None]
    diag_bias = np.where(np.arange(SEL_KC)[None, :, None] <= shift + qrel, 0.0, MASK_VALUE).astype(np.float32)
    return (jnp.asarray(pselt, BF16), jnp.asarray(qout, BF16), jnp.asarray(overlap_t, BF16),
            jnp.asarray(win_bias), jnp.asarray(diag_bias))


def _masked_softmax0(s, allowed):
    s = jnp.where(allowed, s, MASK_VALUE)
    e = jnp.exp(s - jnp.max(s, axis=0, keepdims=True))
    p = e / jnp.sum(e, axis=0, keepdims=True)
    return jnp.where(allowed, p, 0.0)


def _nsa_body(q_ref, kv_ref, kcvc_ref, vct_ref, gate_ref, pselt_ref, qout_ref, ovt_ref, wb_ref, db_ref,
              o_ref, vst_ref, vwt_ref, sb_ref, m_ref, l_ref, acc_ref, *, n_top, n_sel):
    qb = pl.program_id(2)
    rep, qn, hd = NSA_REP, Q_BLOCK, HEAD_DIM
    nq = rep * qn
    blocks_per_chunk = SEL_KC // SEL_BLOCK

    @pl.when(qb == 0)
    def _():
        for cc in range(vst_ref.shape[0]):
            blk = kv_ref[cc * SEL_KC:(cc + 1) * SEL_KC, LANES:2 * LANES].astype(F32)
            vst_ref[cc] = blk.T[hd:, :].astype(BF16)
        for cc in range(vwt_ref.shape[0]):
            blk = kv_ref[cc * qn:(cc + 1) * qn, 2 * LANES:3 * LANES].astype(F32)
            vwt_ref[cc] = blk.T[hd:, :].astype(BF16)

    q2 = q_ref[...]
    qst = jnp.concatenate([_dot_nt(pselt_ref[r], q2) for r in range(rep)], axis=1).astype(BF16)
    tq = qb * qn + lax.broadcasted_iota(jnp.int32, (1, qn), 1)
    tq4 = jnp.concatenate([tq] * rep, axis=1)

    ncp = kcvc_ref.shape[0]
    s = _dot(kcvc_ref[...], qst)
    n_i = lax.broadcasted_iota(jnp.int32, (ncp, 1), 0)
    p = _masked_softmax0(s, (n_i * CMP_STRIDE + (CMP_BLOCK - 1)) <= tq4)
    o_cmp = _dot(vct_ref[...], p.astype(BF16))
    psum = p[:, 0:qn]
    for r in range(1, rep):
        psum = psum + p[:, r * qn:(r + 1) * qn]
    imp = _dot_f32_rhs(ovt_ref[...], psum)

    j = lax.broadcasted_iota(jnp.int32, (n_sel, 1), 0)
    cur = tq // SEL_BLOCK
    forced = (j == 0) | (j == cur) | (j == cur - 1)
    score = jnp.where(forced, FORCE_VALUE, jnp.where(j * SEL_BLOCK <= tq, imp, MASK_VALUE))
    cnt = jnp.zeros((n_sel, qn), F32)
    for jp in range(n_sel):
        row = score[jp:jp + 1, :]
        earlier = jnp.where(j > jp, 1.0, 0.0)
        cnt = cnt + jnp.where(row > score, 1.0, jnp.where(row == score, earlier, 0.0))
    selbias = jnp.where(cnt < n_top, 0.0, MASK_VALUE)
    selbias = jnp.concatenate([selbias] * rep, axis=1)
    for cc in range(n_sel // blocks_per_chunk):
        sb_ref[cc] = selbias[cc * blocks_per_chunk:(cc + 1) * blocks_per_chunk, :]

    m_ref[...] = jnp.full(m_ref.shape, MASK_VALUE, F32)
    l_ref[...] = jnp.zeros(l_ref.shape, F32)
    acc_ref[...] = jnp.zeros(acc_ref.shape, F32)

    def sel_chunk(c, diagonal):
        start = pl.multiple_of(c * SEL_KC, SEL_KC)
        sc = _dot(kv_ref[pl.ds(start, SEL_KC), LANES:2 * LANES], qst)
        bias = sb_ref[c]
        sc = jnp.concatenate([sc[b * SEL_BLOCK:(b + 1) * SEL_BLOCK, :] + bias[b:b + 1, :]
                              for b in range(blocks_per_chunk)], axis=0)
        if diagonal:
            sc = sc + db_ref[qb % (SEL_KC // qn)]
        m_old = m_ref[...]
        m_new = jnp.maximum(m_old, jnp.max(sc, axis=0, keepdims=True))
        alpha = jnp.exp(m_old - m_new)
        pc = jnp.exp(sc - m_new)
        l_ref[...] = alpha * l_ref[...] + jnp.sum(pc, axis=0, keepdims=True)
        acc_ref[...] = alpha * acc_ref[...] + _dot(vst_ref[c], pc.astype(BF16))
        m_ref[...] = m_new

    last = (qb * qn) // SEL_KC

    def full_chunk(c, carry):
        sel_chunk(c, False)
        return carry

    lax.fori_loop(0, last, full_chunk, 0)
    sel_chunk(last, True)
    o_sel = acc_ref[...] / l_ref[...]

    c0 = jnp.maximum(qb - WINDOW // qn, 0)
    wstart = pl.multiple_of(c0 * qn, qn)
    sw = _dot(kv_ref[pl.ds(wstart, WIN_KEYS), 2 * LANES:3 * LANES], qst)
    sw = sw + wb_ref[jnp.minimum(qb, WINDOW // qn)]
    ew = jnp.exp(sw - jnp.max(sw, axis=0, keepdims=True))
    lw = jnp.sum(ew, axis=0, keepdims=True)
    pw = ew.astype(BF16)
    o_win = _dot(vwt_ref[c0], pw[0:qn, :])
    for cc in range(1, WIN_KEYS // qn):
        o_win = o_win + _dot(vwt_ref[c0 + cc], pw[cc * qn:(cc + 1) * qn, :])
    o_win = o_win / lw

    gs = _sigmoid(gate_ref[...])
    out = None
    for r in range(rep):
        sl = slice(r * qn, (r + 1) * qn)
        o_r = (gs[3 * r:3 * r + 1, :] * o_cmp[:, sl] + gs[3 * r + 1:3 * r + 2, :] * o_sel[:, sl]
               + gs[3 * r + 2:3 * r + 3, :] * o_win[:, sl])
        term = _dot(o_r.T.astype(BF16), qout_ref[r])
        out = term if out is None else out + term
    o_ref[...] = out.astype(o_ref.dtype)


def _nsa(q, kv, kcvc, vct, gates_t, seq):
    t = q.shape[0]
    bsz = t // seq
    nqb = seq // Q_BLOCK
    n_sel = seq // SEL_BLOCK
    n_top = min(SEL_TOP, n_sel)
    ncp = seq // CMP_STRIDE
    assert seq >= WIN_KEYS and seq % SEL_KC == 0 and SEL_KC % Q_BLOCK == 0 and n_sel % 8 == 0
    pselt, qout, overlap_t, win_bias, diag_bias = _nsa_consts(seq)
    gw = NSA_REP * HEAD_DIM
    nq = NSA_REP * Q_BLOCK
    nck = seq // SEL_KC
    body = functools.partial(_nsa_body, n_top=n_top, n_sel=n_sel)
    return pl.pallas_call(
        body,
        grid=(bsz, NSA_GROUPS, nqb),
        in_specs=[pl.BlockSpec((Q_BLOCK, gw), lambda b, g, i: (b * nqb + i, g)),
                  pl.BlockSpec((seq, 3 * LANES), lambda b, g, i: (b, g)),
                  pl.BlockSpec((None, None, ncp, LANES), lambda b, g, i: (b, g, 0, 0)),
                  pl.BlockSpec((None, None, HEAD_DIM, ncp), lambda b, g, i: (b, g, 0, 0)),
                  pl.BlockSpec((None, None, 3 * NSA_REP, Q_BLOCK), lambda b, g, i: (b, g, 0, i)),
                  pl.BlockSpec(pselt.shape, lambda b, g, i: (0, 0, 0)),
                  pl.BlockSpec(qout.shape, lambda b, g, i: (0, 0, 0)),
                  pl.BlockSpec(overlap_t.shape, lambda b, g, i: (0, 0)),
                  pl.BlockSpec(win_bias.shape, lambda b, g, i: (0, 0, 0)),
                  pl.BlockSpec(diag_bias.shape, lambda b, g, i: (0, 0, 0))],
        out_specs=pl.BlockSpec((Q_BLOCK, gw), lambda b, g, i: (b * nqb + i, g)),
        out_shape=jax.ShapeDtypeStruct((t, NSA_HEADS * HEAD_DIM), BF16),
        scratch_shapes=[pltpu.VMEM((nck, HEAD_DIM, SEL_KC), BF16),
                        pltpu.VMEM((nqb, HEAD_DIM, Q_BLOCK), BF16),
                        pltpu.VMEM((nck, SEL_KC // SEL_BLOCK, nq), F32),
                        pltpu.VMEM((1, nq), F32),
                        pltpu.VMEM((1, nq), F32),
                        pltpu.VMEM((HEAD_DIM, nq), F32)],
        compiler_params=_params("arbitrary", "arbitrary", "arbitrary"),
        name="nsa",
    )(q, kv, kcvc, vct, gates_t, pselt, qout, overlap_t, win_bias, diag_bias)


def _dn_body(qkv_ref, sm_ref, z_ref, cw_ref, hp_ref, ng_ref, ones_ref, tri_ref, o_ref,
             state_ref, prev_ref):
    c = pl.program_id(1)
    ch, hd = DN_CHUNK, HEAD_DIM
    width = DN_HEADS * hd
    nb = qkv_ref.shape[0]
    npair = DN_HEADS // 2
    units = [(b, p) for b in range(nb) for p in range(npair)]

    @pl.when(c == 0)
    def _():
        state_ref[...] = jnp.zeros(state_ref.shape, F32)
        prev_ref[...] = jnp.zeros(prev_ref.shape, F32)

    lane = lax.broadcasted_iota(jnp.int32, (1, LANES), 1)
    first = lane < hd
    ri = lax.broadcasted_iota(jnp.int32, (2 * ch, 1), 0)
    ci = lax.broadcasted_iota(jnp.int32, (1, 2 * ch), 1)
    same = (ri // ch) == (ci // ch)
    causal = same & ((ri % ch) >= (ci % ch))
    strict = same & ((ri % ch) > (ci % ch))
    blockdiag = (lax.broadcasted_iota(jnp.int32, (LANES, 1), 0) // hd) == (lane // hd)
    ones_blk = ones_ref[...]

    def stack(v):
        return jnp.concatenate([jnp.where(first, v, 0.0), jnp.where(first, 0.0, v)], axis=0)

    def fold(v):
        return v[:ch] + v[ch:]

    def head_sumsq(v):
        return _dot((v * v).astype(BF16), ones_blk)

    ys, betas, gcs = [], [], []
    for b in range(nb):
        x = qkv_ref[b].astype(F32)
        xe = jnp.concatenate([prev_ref[b], x], axis=0)
        y = cw_ref[DN_CONV - 1:DN_CONV, :] * x
        for tap in range(DN_CONV - 1):
            y = y + cw_ref[tap:tap + 1, :] * pltpu.roll(xe, DN_CONV - 1 - tap, 0)[8:, :]
        prev_ref[b] = x[ch - 8:, :]
        ys.append(y * _sigmoid(y))
        sm = sm_ref[b]
        betas.append(_sigmoid(sm))
        xa = sm + hp_ref[1:2, :]
        softplus = jnp.maximum(xa, 0.0) + jnp.log(1.0 + jnp.exp(-jnp.abs(xa)))
        gcs.append(_dot_f32_rhs(tri_ref[...], -jnp.exp(hp_ref[0:1, :]) * softplus))

    def bc(tile, base, p):
        return jnp.where(first, tile[:, base + 2 * p:base + 2 * p + 1], tile[:, base + 2 * p + 1:base + 2 * p + 2])

    qn, kn, vb, kb, gc, egc, dec = {}, {}, {}, {}, {}, {}, {}
    for u in units:
        b, p = u
        y = ys[b]
        qp = y[:, p * LANES:(p + 1) * LANES]
        kp = y[:, width + p * LANES:width + (p + 1) * LANES]
        vp = y[:, 2 * width + p * LANES:2 * width + (p + 1) * LANES]
        qn[u] = qp * lax.rsqrt(head_sumsq(qp) + EPS) * (hd ** -0.5)
        kn[u] = kp * lax.rsqrt(head_sumsq(kp) + EPS)
        beta = bc(betas[b], SM_BETA, p)
        gc[u] = bc(gcs[b], SM_A, p)
        egc[u] = jnp.exp(gc[u])
        kb[u] = kn[u] * beta
        vb[u] = vp * beta
        gcol = jnp.concatenate([gcs[b][:, SM_A + 2 * p:SM_A + 2 * p + 1],
                                gcs[b][:, SM_A + 2 * p + 1:SM_A + 2 * p + 2]], axis=0)
        gmat = jnp.broadcast_to(gcol, (2 * ch, 2 * ch))
        dec[u] = jnp.where(causal, jnp.exp(jnp.where(causal, gmat - gmat.T, 0.0)), 0.0)

    nmat, attn, sol = {}, {}, {}
    for u in units:
        ks = jnp.concatenate([kn[u], kn[u]], axis=0).astype(BF16)
        nmat[u] = -(_dot_nt(stack(kb[u]).astype(BF16), ks) * jnp.where(strict, dec[u], 0.0))
        attn[u] = (_dot_nt(stack(qn[u]).astype(BF16), ks) * dec[u]).astype(BF16)
        sol[u] = jnp.concatenate([stack(vb[u]), stack(kb[u] * egc[u])], axis=1)

    n_fac = int(math.log2(ch))
    for it in range(n_fac):
        for u in units:
            nb16 = nmat[u].astype(BF16)
            sol[u] = sol[u] + _dot(nb16, sol[u].astype(BF16))
            if it < n_fac - 1:
                nmat[u] = _dot(nb16, nb16)

    v_new, st = {}, {}
    for u in units:
        b, p = u
        st[u] = state_ref[b, p]
        v_new[u] = fold(sol[u][:, :LANES]) - _dot(fold(sol[u][:, LANES:]).astype(BF16), st[u].astype(BF16))

    o = {}
    for u in units:
        b, p = u
        glast = gc[u][ch - 1:ch, :]
        o[u] = (_dot((qn[u] * egc[u]).astype(BF16), st[u].astype(BF16))
                + fold(_dot(attn[u], stack(v_new[u]).astype(BF16))))
        k_dec = kn[u] * jnp.exp(glast - gc[u])
        upd = _dot(k_dec.T.astype(BF16), v_new[u].astype(BF16))
        state_ref[b, p] = st[u] * jnp.exp(glast) + jnp.where(blockdiag, upd, 0.0)

    for u in units:
        b, p = u
        sl = slice(p * LANES, (p + 1) * LANES)
        on = o[u] * lax.rsqrt(head_sumsq(o[u]) * (1.0 / hd) + EPS) * ng_ref[...]
        zp = z_ref[b, :, sl].astype(F32)
        o_ref[b, :, sl] = (on * (zp * _sigmoid(zp))).astype(o_ref.dtype)


def _dn(qkv, sm, z, conv_w, hp, ng, seq):
    t = qkv.shape[0]
    bsz = t // seq
    nb = 2 if bsz % 2 == 0 else 1
    nc = seq // DN_CHUNK
    width = DN_HEADS * HEAD_DIM
    ones_blk = jnp.asarray(np.kron(np.eye(2), np.ones((HEAD_DIM, HEAD_DIM))), BF16)
    tri = jnp.asarray(np.tril(np.ones((DN_CHUNK, DN_CHUNK))), BF16)
    y = pl.pallas_call(
        _dn_body,
        grid=(bsz // nb, nc),
        in_specs=[pl.BlockSpec((nb, DN_CHUNK, 3 * width), lambda b, c: (b, c, 0)),
                  pl.BlockSpec((nb, DN_CHUNK, LANES), lambda b, c: (b, c, 0)),
                  pl.BlockSpec((nb, DN_CHUNK, width), lambda b, c: (b, c, 0)),
                  pl.BlockSpec(conv_w.shape, lambda b, c: (0, 0)),
                  pl.BlockSpec(hp.shape, lambda b, c: (0, 0)),
                  pl.BlockSpec(ng.shape, lambda b, c: (0, 0)),
                  pl.BlockSpec(ones_blk.shape, lambda b, c: (0, 0)),
                  pl.BlockSpec(tri.shape, lambda b, c: (0, 0))],
        out_specs=pl.BlockSpec((nb, DN_CHUNK, width), lambda b, c: (b, c, 0)),
        out_shape=jax.ShapeDtypeStruct((bsz, seq, width), BF16),
        scratch_shapes=[pltpu.VMEM((nb, DN_HEADS // 2, LANES, LANES), F32),
                        pltpu.VMEM((nb, 8, 3 * width), F32)],
        compiler_params=_params("arbitrary", "arbitrary"),
        name="dn",
    )(qkv.reshape(bsz, seq, 3 * width), sm.reshape(bsz, seq, LANES), z.reshape(bsz, seq, width),
      conv_w, hp, ng, ones_blk, tri)
    return y.reshape(t, width)


def _merge_body(yn_ref, yd_ref, mg_ref, x_ref, mod_ref, gf_ref, wb_ref, wo_ref, rwh_ref, rwl_ref,
                rb_ref, tri_ref, x1_ref, h_ref, rt_ref, cnt_ref):
    i = pl.program_id(0)
    d = x_ref.shape[1]

    @pl.when(i == 0)
    def _():
        cnt_ref[...] = jnp.zeros(cnt_ref.shape, F32)

    br0 = _dot(yn_ref[...], wb_ref[0])
    br1 = _dot(yd_ref[...], wb_ref[1])
    mixin = (_sigmoid(mg_ref[:, :d].astype(F32)) * br0 + _sigmoid(mg_ref[:, d:].astype(F32)) * br1)
    mix = _dot(mixin.astype(BF16), wo_ref[...])
    x1 = x_ref[...] + mod_ref[2:3, :] * mix
    x1_ref[...] = x1
    h = _rms_mod(x1, gf_ref[...], mod_ref[3:4, :], mod_ref[4:5, :])
    h_ref[...] = h

    hh = h.astype(BF16)
    hl = (h - hh.astype(F32)).astype(BF16)
    logits = _dot(hh, rwh_ref[...]) + _dot(hh, rwl_ref[...]) + _dot(hl, rwh_ref[...]) + rb_ref[...]
    lane = lax.broadcasted_iota(jnp.int32, (1, LANES), 1)
    cur = jnp.where(lane < N_EXPERTS, logits, PAD_SCORE)
    vals, idxs = [], []
    for _ in range(TOP_K):
        m = jnp.max(cur, axis=-1, keepdims=True)
        ix = jnp.min(jnp.where(cur == m, lane, LANES), axis=-1, keepdims=True)
        vals.append(m)
        idxs.append(ix)
        cur = jnp.where(lane == ix, PAD_SCORE, cur)
    es = [jnp.exp(v - vals[0]) for v in vals]
    den = es[0] + es[1] + es[2] + es[3]

    onehot = jnp.zeros(logits.shape, F32)
    for ix in idxs:
        onehot = onehot + jnp.where(lane == ix, 1.0, 0.0)
    before = _dot(tri_ref[...], onehot.astype(BF16)) + cnt_ref[...]
    cnt_ref[...] = cnt_ref[...] + jnp.sum(onehot, axis=0, keepdims=True)

    rt = jnp.zeros(logits.shape, F32)
    for k in range(TOP_K):
        rank = jnp.sum(jnp.where(lane == idxs[k], before, 0.0), axis=-1, keepdims=True)
        rt = jnp.where(lane == RT_IDX + k, idxs[k].astype(F32), rt)
        rt = jnp.where(lane == RT_W + k, es[k] / den, rt)
        rt = jnp.where(lane == RT_RANK + k, rank, rt)
    rt_ref[...] = rt


def _merge(y_nsa, y_dn, mg, x2, mod3, g_ffn, wb, wo, rwh, rwl, rb, seq):
    t, d = x2.shape
    tm = 256
    per_b = seq // tm
    hw = y_nsa.shape[1]
    tri = jnp.asarray(np.tril(np.ones((tm, tm)), -1), BF16)
    return pl.pallas_call(
        _merge_body,
        grid=(t // tm,),
        in_specs=[pl.BlockSpec((tm, hw), lambda i: (i, 0)),
                  pl.BlockSpec((tm, hw), lambda i: (i, 0)),
                  pl.BlockSpec((tm, 2 * d), lambda i: (i, 0)),
                  pl.BlockSpec((tm, d), lambda i: (i, 0)),
                  pl.BlockSpec((None, 6, d), lambda i: (i // per_b, 0, 0)),
                  pl.BlockSpec((1, d), lambda i: (0, 0)),
                  pl.BlockSpec(wb.shape, lambda i: (0, 0, 0)),
                  pl.BlockSpec(wo.shape, lambda i: (0, 0)),
                  pl.BlockSpec(rwh.shape, lambda i: (0, 0)),
                  pl.BlockSpec(rwl.shape, lambda i: (0, 0)),
                  pl.BlockSpec(rb.shape, lambda i: (0, 0)),
                  pl.BlockSpec(tri.shape, lambda i: (0, 0))],
        out_specs=[pl.BlockSpec((tm, d), lambda i: (i, 0)),
                   pl.BlockSpec((tm, d), lambda i: (i, 0)),
                   pl.BlockSpec((tm, LANES), lambda i: (i, 0)),
                   pl.BlockSpec((1, LANES), lambda i: (0, 0))],
        out_shape=[jax.ShapeDtypeStruct((t, d), F32),
                   jax.ShapeDtypeStruct((t, d), F32),
                   jax.ShapeDtypeStruct((t, LANES), F32),
                   jax.ShapeDtypeStruct((1, LANES), F32)],
        compiler_params=_params("arbitrary"),
        name="merge",
    )(y_nsa, y_dn, mg, x2, mod3, g_ffn.reshape(1, d), wb, wo, rwh, rwl, rb, tri)


def _row_copy(src, src_row, dst, dst_row, sem):
    return pltpu.make_async_copy(src.at[pl.ds(src_row, 1), :], dst.at[pl.ds(dst_row, 1), :], sem)


def _scatter_body(pad_lo_ref, pad_hi_ref, slot_ref, h_ref, xs_ref, zero_ref, sem, zsem):
    tm = h_ref.shape[0]

    @pl.when(pl.program_id(0) == 0)
    def _():
        zero_ref[...] = jnp.zeros(zero_ref.shape, F32)
        for e in range(N_EXPERTS):
            def zissue(r, carry):
                _row_copy(zero_ref, 0, xs_ref, r, zsem).start()
                return carry

            def zdrain(r, carry):
                _row_copy(zero_ref, 0, xs_ref, r, zsem).wait()
                return carry

            lax.fori_loop(pad_lo_ref[e], pad_hi_ref[e], zissue, 0)
            lax.fori_loop(pad_lo_ref[e], pad_hi_ref[e], zdrain, 0)

    def issue(r, carry):
        for k in range(TOP_K):
            _row_copy(h_ref, r, xs_ref, slot_ref[0, r * TOP_K + k], sem).start()
        return carry

    lax.fori_loop(0, tm, issue, 0)

    def drain(r, carry):
        for k in range(TOP_K):
            _row_copy(h_ref, r, xs_ref, slot_ref[0, r * TOP_K + k], sem).wait()
        return carry

    lax.fori_loop(0, tm, drain, 0)


def _scatter(pad_lo, pad_hi, slots3, h, n_rows):
    t, d = h.shape
    tm = SCATTER_TILE
    grid_spec = pltpu.PrefetchScalarGridSpec(
        num_scalar_prefetch=2,
        grid=(t // tm,),
        in_specs=[pl.BlockSpec((None, 1, tm * TOP_K), lambda i, lo, hi: (i, 0, 0), memory_space=pltpu.SMEM),
                  pl.BlockSpec((tm, d), lambda i, lo, hi: (i, 0))],
        out_specs=pl.BlockSpec(memory_space=pl.ANY),
        scratch_shapes=[pltpu.VMEM((8, d), F32), pltpu.SemaphoreType.DMA(()), pltpu.SemaphoreType.DMA(())],
    )
    return pl.pallas_call(
        _scatter_body,
        grid_spec=grid_spec,
        out_shape=jax.ShapeDtypeStruct((n_rows, d), F32),
        compiler_params=_params("arbitrary"),
        name="scatter",
    )(pad_lo, pad_hi, slots3, h)


def _experts_body(te_ref, nv_ref, xs_ref, w1_ref, b1_ref, w2_ref, b2_ref, ys_ref):
    i = pl.program_id(0)
    f = w2_ref.shape[0]

    @pl.when(i < nv_ref[0])
    def _():
        xb = xs_ref[...].astype(BF16)
        u = _dot(xb, w1_ref[...]) + b1_ref[...]
        x_glu = jnp.minimum(u[:, :f], SWIGLU_LIMIT)
        x_lin = jnp.clip(u[:, f:], -SWIGLU_LIMIT, SWIGLU_LIMIT)
        act = x_glu * _sigmoid(SWIGLU_ALPHA * x_glu) * (x_lin + 1.0)
        ys_ref[...] = _dot(act.astype(BF16), w2_ref[...]) + b2_ref[...]

    @pl.when(i >= nv_ref[0])
    def _():
        ys_ref[...] = jnp.zeros(ys_ref.shape, F32)


def _experts(tile_expert, n_valid, xs, w1, b1, w2, b2):
    p, d = xs.shape
    tm = EXPERT_TILE
    f = w2.shape[1]
    grid_spec = pltpu.PrefetchScalarGridSpec(
        num_scalar_prefetch=2,
        grid=(p // tm,),
        in_specs=[pl.BlockSpec((tm, d), lambda i, te, nv: (jnp.minimum(i, nv[0] - 1), 0)),
                  pl.BlockSpec((None, d, 2 * f), lambda i, te, nv: (te[i], 0, 0)),
                  pl.BlockSpec((None, 1, 2 * f), lambda i, te, nv: (te[i], 0, 0)),
                  pl.BlockSpec((None, f, d), lambda i, te, nv: (te[i], 0, 0)),
                  pl.BlockSpec((None, 1, d), lambda i, te, nv: (te[i], 0, 0))],
        out_specs=pl.BlockSpec((tm, d), lambda i, te, nv: (i, 0)),
    )
    return pl.pallas_call(
        _experts_body,
        grid_spec=grid_spec,
        out_shape=jax.ShapeDtypeStruct((p, d), F32),
        compiler_params=_params("arbitrary"),
        name="experts",
    )(tile_expert, n_valid, xs, w1, b1, w2, b2)


def _combine_body(slot_ref, rt_ref, x1_ref, mod_ref, g_ref, ys_ref, o_ref, buf_ref, sem):
    tm = x1_ref.shape[0]

    def issue(r, carry):
        for k in range(TOP_K):
            _row_copy(ys_ref, slot_ref[0, r * TOP_K + k], buf_ref.at[k], r, sem).start()
        return carry

    lax.fori_loop(0, tm, issue, 0)

    def drain(r, carry):
        for k in range(TOP_K):
            _row_copy(ys_ref, slot_ref[0, r * TOP_K + k], buf_ref.at[k], r, sem).wait()
        return carry

    lax.fori_loop(0, tm, drain, 0)

    rt = rt_ref[...]
    moe = rt[:, RT_W:RT_W + 1] * buf_ref[0]
    for k in range(1, TOP_K):
        moe = moe + rt[:, RT_W + k:RT_W + k + 1] * buf_ref[k]
    x2 = x1_ref[...] + mod_ref[5:6, :] * moe
    ms = jnp.mean(x2 * x2, axis=-1, keepdims=True)
    o_ref[...] = x2 * lax.rsqrt(ms + EPS) * g_ref[...]


def _combine(slots3, rt, x1, mod3, g_final, ys, seq):
    t, d = x1.shape
    tm = SCATTER_TILE
    per_b = seq // tm
    return pl.pallas_call(
        _combine_body,
        grid=(t // tm,),
        in_specs=[pl.BlockSpec((None, 1, tm * TOP_K), lambda i: (i, 0, 0), memory_space=pltpu.SMEM),
                  pl.BlockSpec((tm, LANES), lambda i: (i, 0)),
                  pl.BlockSpec((tm, d), lambda i: (i, 0)),
                  pl.BlockSpec((None, 6, d), lambda i: (i // per_b, 0, 0)),
                  pl.BlockSpec((1, d), lambda i: (0, 0)),
                  pl.BlockSpec(memory_space=pl.ANY)],
        out_specs=pl.BlockSpec((tm, d), lambda i: (i, 0)),
        out_shape=jax.ShapeDtypeStruct((t, d), F32),
        scratch_shapes=[pltpu.VMEM((TOP_K, tm, d), F32), pltpu.SemaphoreType.DMA(())],
        compiler_params=_params("arbitrary"),
        name="combine",
    )(slots3, rt, x1, mod3, g_final.reshape(1, d), ys)


def _pad_lanes(v, offset):
    out = jnp.zeros((1, LANES), F32)
    return out.at[0, offset:offset + v.shape[0]].set(v.astype(F32))


def kernel(x, c, w_ada, b_ada, g_norm_mix, w_in, cmp_pe_k, cmp_pe_v, cmp_w1, cmp_b1, cmp_w2,
           dn_conv_w, dn_a_log, dn_dt_bias, dn_norm_g, w_branch, w_out, g_norm_ffn,
           router_w, router_b, exp_w1, exp_b1, exp_w2, exp_b2, final_norm_g):
    bsz, seq, d = x.shape
    t = bsz * seq
    depth = w_ada.shape[0]
    assert depth == 1, "the final norm is fused into the last layer's combine step"
    x2 = x.reshape(t, d)
    cols = _in_columns()
    out = None
    for l in range(depth):
        mod3 = _ada(c, w_ada[l], b_ada[l]).reshape(bsz, 6, d)

        w_big = jnp.where(jnp.asarray(cols >= 0)[None, :], w_in[l][:, np.maximum(cols, 0)], 0.0).astype(BF16)
        q, kv, dnqkv, z, mg, sm = _inproj(x2, mod3, g_norm_mix[l], w_big, seq)

        nrow = seq // CMP_STRIDE
        src = kv.reshape(bsz, nrow, CMP_STRIDE, NSA_GROUPS, 6, HEAD_DIM)[:, :, :, :, 0:2, :]
        src = src.transpose(0, 3, 4, 1, 2, 5).reshape(bsz, NSA_GROUPS, 2, nrow, CMP_STRIDE * HEAD_DIM)
        pe = jnp.stack([cmp_pe_k[l], cmp_pe_v[l]]).reshape(2, 1, CMP_BLOCK * HEAD_DIM)
        pe = jnp.broadcast_to(pe, (2, 8, CMP_BLOCK * HEAD_DIM)).astype(BF16)
        w2p = jnp.zeros((2, CMP_HIDDEN, LANES), F32)
        w2p = w2p.at[0, :, :HEAD_DIM].set(cmp_w2[l, 0]).at[1, :, HEAD_DIM:].set(cmp_w2[l, 1]).astype(BF16)
        kcvc, vct = _cmp(src, cmp_w1[l].astype(BF16), pe, cmp_b1[l].reshape(2, 1, CMP_HIDDEN), w2p,
                         cmp_w2[l, 1].T.astype(BF16))

        gates_t = sm[:, SM_GATE:SM_GATE + 3 * NSA_HEADS].reshape(bsz, seq, NSA_GROUPS, 3 * NSA_REP)
        gates_t = gates_t.transpose(0, 2, 3, 1)
        y_nsa = _nsa(q, kv, kcvc, vct, gates_t, seq)

        hp = jnp.concatenate([_pad_lanes(dn_a_log[l], SM_A), _pad_lanes(dn_dt_bias[l], SM_A),
                              jnp.zeros((6, LANES), F32)], axis=0)
        ng = jnp.tile(dn_norm_g[l].reshape(1, HEAD_DIM), (1, 2))
        y_dn = _dn(dnqkv, sm, z, dn_conv_w[l], hp, ng, seq)

        rw = jnp.zeros((d, LANES), F32).at[:, :N_EXPERTS].set(router_w[l])
        rwh = rw.astype(BF16)
        rwl = (rw - rwh.astype(F32)).astype(BF16)
        x1, h, rt, cnt = _merge(y_nsa, y_dn, mg, x2, mod3, g_norm_ffn[l], w_branch[l].astype(BF16),
                                w_out[l].astype(BF16), rwh, rwl, _pad_lanes(router_b[l], 0), seq)

        counts = cnt[0, :N_EXPERTS].astype(jnp.int32)
        tiles_per = (counts + EXPERT_TILE - 1) // EXPERT_TILE
        tile_end = jnp.cumsum(tiles_per)
        offs = (tile_end - tiles_per) * EXPERT_TILE
        n_rows = t * TOP_K + N_EXPERTS * EXPERT_TILE
        n_tiles = n_rows // EXPERT_TILE
        idx = rt[:, RT_IDX:RT_IDX + TOP_K].astype(jnp.int32)
        rank = rt[:, RT_RANK:RT_RANK + TOP_K].astype(jnp.int32)
        slots = offs[idx] + rank
        slots3 = slots.reshape(t // SCATTER_TILE, 1, SCATTER_TILE * TOP_K)
        tile_ids = jnp.arange(n_tiles, dtype=jnp.int32)
        tile_expert = jnp.minimum(jnp.sum((tile_ids[:, None] >= tile_end[None, :]).astype(jnp.int32), axis=1),
                                  N_EXPERTS - 1).astype(jnp.int32)
        n_valid = tile_end[-1:].astype(jnp.int32)

        xs = _scatter(offs + counts, offs + tiles_per * EXPERT_TILE, slots3, h, n_rows)
        ys = _experts(tile_expert, n_valid, xs, exp_w1[l].astype(BF16),
                      exp_b1[l].reshape(N_EXPERTS, 1, -1), exp_w2[l].astype(BF16),
                      exp_b2[l].reshape(N_EXPERTS, 1, -1))
        out = _combine(slots3, rt, x1, mod3, final_norm_g, ys, seq)
    return out.reshape(bsz, seq, d)
```

```python
import functools
import math

import numpy as np
import jax
import jax.numpy as jnp
from jax import lax
from jax.experimental import pallas as pl
from jax.experimental.pallas import tpu as pltpu

F32 = jnp.float32
BF16 = jnp.bfloat16

HEAD_DIM = 64
NSA_HEADS = 8
NSA_GROUPS = 2
NSA_REP = NSA_HEADS // NSA_GROUPS
CMP_BLOCK = 32
CMP_STRIDE = 16
CMP_HIDDEN = 256
SEL_BLOCK = 64
SEL_TOP = 16
WINDOW = 512
Q_BLOCK = 128
DN_HEADS = 8
DN_CONV = 4
DN_CHUNK = 64
N_EXPERTS = 32
TOP_K = 4
SWIGLU_LIMIT = 7.0
SWIGLU_ALPHA = 1.702
EPS = 1e-6
MASK_VALUE = -1e30
FORCE_VALUE = 1e9
PAD_SCORE = -3e38

LANES = 128
VMEM_LIMIT = 56 * 2 ** 20

SEL_KC = 512
WIN_KEYS = WINDOW + Q_BLOCK
EXPERT_TILE = 512
SCATTER_TILE = 256

SM_GATE, SM_BETA, SM_A = 0, 24, 32
RT_IDX, RT_W, RT_RANK = 0, 4, 8


def _dot(a, b):
    return jnp.dot(a, b, preferred_element_type=F32)


def _dot_nt(a, b):
    return lax.dot_general(a, b, (((1,), (1,)), ((), ())), preferred_element_type=F32)


def _split3(x):
    hi = x.astype(BF16)
    r1 = x - hi.astype(F32)
    mid = r1.astype(BF16)
    lo = (r1 - mid.astype(F32)).astype(BF16)
    return hi, mid, lo


def _dot_f32_lhs(x, w_bf16):
    hi, mid, lo = _split3(x)
    return _dot(hi, w_bf16) + _dot(mid, w_bf16) + _dot(lo, w_bf16)


def _dot_f32_rhs(w_bf16, x):
    hi, mid, lo = _split3(x)
    return _dot(w_bf16, hi) + _dot(w_bf16, mid) + _dot(w_bf16, lo)


def _sigmoid(x):
    return 1.0 / (1.0 + jnp.exp(-x))


def _params(*sem):
    return pltpu.CompilerParams(dimension_semantics=sem, vmem_limit_bytes=VMEM_LIMIT)


def _ada_body(c_ref, w_ref, b_ref, o_ref):
    c = c_ref[...]
    a = (c * _sigmoid(c)).astype(BF16)
    o_ref[...] = _dot(a, w_ref[...].astype(BF16)) + b_ref[...]


def _ada(c, w, b):
    bsz, d = c.shape
    n = w.shape[1]
    tn = 1024
    return pl.pallas_call(
        _ada_body,
        grid=(n // tn,),
        in_specs=[pl.BlockSpec((bsz, d), lambda j: (0, 0)),
                  pl.BlockSpec((d, tn), lambda j: (0, j)),
                  pl.BlockSpec((1, tn), lambda j: (0, j))],
        out_specs=pl.BlockSpec((bsz, tn), lambda j: (0, j)),
        out_shape=jax.ShapeDtypeStruct((bsz, n), F32),
        compiler_params=_params("arbitrary"),
        name="ada",
    )(c, w, b.reshape(1, n))


IN_SEGS = (("q", 512, BF16), ("kv", 768, BF16), ("dn", 1536, BF16),
           ("z", 512, BF16), ("mg", 2048, BF16), ("sm", LANES, F32))


def _in_columns():
    q0 = 0
    kv0 = q0 + 512
    gate0 = kv0 + 768
    dn0 = gate0 + 24
    beta0 = dn0 + 1536
    a0 = beta0 + 8
    z0 = a0 + 8
    mg0 = z0 + 512
    cols = list(range(q0, q0 + 512))
    for g in range(NSA_GROUPS):
        for i in range(6):
            base = kv0 + i * NSA_GROUPS * HEAD_DIM + g * HEAD_DIM
            cols += list(range(base, base + HEAD_DIM))
    cols += list(range(dn0, dn0 + 1536))
    cols += list(range(z0, z0 + 512))
    cols += list(range(mg0, mg0 + 2048))
    small = list(range(gate0, gate0 + 24)) + list(range(beta0, beta0 + 8)) + list(range(a0, a0 + 8))
    cols += small + [-1] * (LANES - len(small))
    return np.asarray(cols, np.int32)


def _rms_mod(x, g, shift, scale):
    ms = jnp.mean(x * x, axis=-1, keepdims=True)
    y = x * lax.rsqrt(ms + EPS) * g
    return y * (1.0 + scale) + shift


def _inproj_body(x_ref, mod_ref, g_ref, w_ref, *out_refs):
    h = _rms_mod(x_ref[...], g_ref[...], mod_ref[0:1, :], mod_ref[1:2, :])
    hb = h.astype(BF16)
    off = 0
    for ref, (_, width, _) in zip(out_refs, IN_SEGS):
        for c0 in range(0, width, 512):
            cw = min(512, width - c0)
            ref[:, c0:c0 + cw] = _dot(hb, w_ref[:, off + c0:off + c0 + cw]).astype(ref.dtype)
        off += width


def _inproj(x2, mod3, g, w_big, seq):
    t, d = x2.shape
    tm = 512
    per_b = seq // tm
    nw = w_big.shape[1]
    return pl.pallas_call(
        _inproj_body,
        grid=(t // tm,),
        in_specs=[pl.BlockSpec((tm, d), lambda i: (i, 0)),
                  pl.BlockSpec((None, 6, d), lambda i: (i // per_b, 0, 0)),
                  pl.BlockSpec((1, d), lambda i: (0, 0)),
                  pl.BlockSpec((d, nw), lambda i: (0, 0))],
        out_specs=[pl.BlockSpec((tm, w), lambda i: (i, 0)) for _, w, _ in IN_SEGS],
        out_shape=[jax.ShapeDtypeStruct((t, w), dt) for _, w, dt in IN_SEGS],
        compiler_params=_params("arbitrary"),
        name="inproj",
    )(x2, mod3, g.reshape(1, d), w_big)


def _cmp_body(src_ref, w1_ref, pe_ref, b1_ref, w2_ref, w2t_ref, o_ref, ot_ref):
    half = CMP_STRIDE * HEAD_DIM
    out = None
    for kind in range(2):
        x = src_ref[kind]
        first = _dot(x, w1_ref[kind, :half, :])
        second = _dot(x, w1_ref[kind, half:, :])
        n = second.shape[0]
        second = pltpu.roll(second, n - 1, 0)
        pew = _dot(pe_ref[kind], w1_ref[kind])[0:1, :]
        pre = first + second + pew + b1_ref[kind]
        hid = (pre * _sigmoid(pre)).astype(BF16)
        term = _dot(hid, w2_ref[kind])
        out = term if out is None else out + term
    o_ref[...] = out.astype(o_ref.dtype)
    ot_ref[...] = _dot_nt(w2t_ref[...], hid).astype(ot_ref.dtype)


def _cmp(src, w1, pe, b1, w2p, w2t):
    bsz, ng, _, nrow, width = src.shape
    return pl.pallas_call(
        _cmp_body,
        grid=(bsz, ng),
        in_specs=[pl.BlockSpec((None, None, 2, nrow, width), lambda b, g: (b, g, 0, 0, 0)),
                  pl.BlockSpec(w1.shape, lambda b, g: (0, 0, 0)),
                  pl.BlockSpec(pe.shape, lambda b, g: (0, 0, 0)),
                  pl.BlockSpec(b1.shape, lambda b, g: (0, 0, 0)),
                  pl.BlockSpec(w2p.shape, lambda b, g: (0, 0, 0)),
                  pl.BlockSpec(w2t.shape, lambda b, g: (0, 0))],
        out_specs=[pl.BlockSpec((None, None, nrow, LANES), lambda b, g: (b, g, 0, 0)),
                   pl.BlockSpec((None, None, HEAD_DIM, nrow), lambda b, g: (b, g, 0, 0))],
        out_shape=[jax.ShapeDtypeStruct((bsz, ng, nrow, LANES), BF16),
                   jax.ShapeDtypeStruct((bsz, ng, HEAD_DIM, nrow), BF16)],
        compiler_params=_params("arbitrary", "arbitrary"),
        name="cmp",
    )(src, w1, pe, b1, w2p, w2t)


def _nsa_consts(seq):
    scale = HEAD_DIM ** -0.5
    gw = NSA_REP * HEAD_DIM
    pselt = np.zeros((NSA_REP, LANES, gw), np.float32)
    qout = np.zeros((NSA_REP, HEAD_DIM, gw), np.float32)
    for r in range(NSA_REP):
        for d in range(HEAD_DIM):
            pselt[r, d, r * HEAD_DIM + d] = scale
            qout[r, d, r * HEAD_DIM + d] = 1.0
    n_cmp = (seq - CMP_BLOCK) // CMP_STRIDE + 1
    n_sel = seq // SEL_BLOCK
    cs = np.arange(n_cmp)[:, None] * CMP_STRIDE
    ss = np.arange(n_sel)[None, :] * SEL_BLOCK
    ov = np.clip(np.minimum(cs + CMP_BLOCK, ss + SEL_BLOCK) - np.maximum(cs, ss), 0, None) / CMP_BLOCK
    overlap_t = np.zeros((n_sel, seq // CMP_STRIDE), np.float32)
    overlap_t[:, :n_cmp] = ov.T
    qrel = np.tile(np.arange(Q_BLOCK), NSA_REP)[None, None, :]
    shift = (np.arange(WINDOW // Q_BLOCK + 1) * Q_BLOCK)[:, None, None]
    delta = shift + qrel - np.arange(WIN_KEYS)[None, :, None]
    win_bias = np.where((delta >= 0) & (delta < WINDOW), 0.0, MASK_VALUE).astype(np.float32)
    shift = (np.arange(SEL_KC // Q_BLOCK) * Q_BLOCK)[:, None, None]
    diag_bias = np.where(np.arange(SEL_KC)[None, :, None] <= shift + qrel, 0.0, MASK_VALUE).astype(np.float32)
    return (jnp.asarray(pselt, BF16), jnp.asarray(qout, BF16), jnp.asarray(overlap_t, BF16),
            jnp.asarray(win_bias), jnp.asarray(diag_bias))


def _masked_softmax0(s, allowed):
    s = jnp.where(allowed, s, MASK_VALUE)
    e = jnp.exp(s - jnp.max(s, axis=0, keepdims=True))
    p = e / jnp.sum(e, axis=0, keepdims=True)
    return jnp.where(allowed, p, 0.0)


def _nsa_body(q_ref, kv_ref, kcvc_ref, vct_ref, gate_ref, pselt_ref, qout_ref, ovt_ref, wb_ref, db_ref,
              o_ref, vst_ref, vwt_ref, sb_ref, m_ref, l_ref, acc_ref, *, n_top, n_sel):
    qb = pl.program_id(2)
    rep, qn, hd = NSA_REP, Q_BLOCK, HEAD_DIM
    nq = rep * qn
    blocks_per_chunk = SEL_KC // SEL_BLOCK

    @pl.when(qb == 0)
    def _():
        for cc in range(vst_ref.shape[0]):
            blk = kv_ref[cc * SEL_KC:(cc + 1) * SEL_KC, LANES:2 * LANES].astype(F32)
            vst_ref[cc] = blk.T[hd:, :].astype(BF16)
        for cc in range(vwt_ref.shape[0]):
            blk = kv_ref[cc * qn:(cc + 1) * qn, 2 * LANES:3 * LANES].astype(F32)
            vwt_ref[cc] = blk.T[hd:, :].astype(BF16)

    q2 = q_ref[...]
    qst = jnp.concatenate([_dot_nt(pselt_ref[r], q2) for r in range(rep)], axis=1).astype(BF16)
    tq = qb * qn + lax.broadcasted_iota(jnp.int32, (1, qn), 1)
    tq4 = jnp.concatenate([tq] * rep, axis=1)

    c0 = jnp.maximum(qb - WINDOW // qn, 0)
    wstart = pl.multiple_of(c0 * qn, qn)
    sw = _dot(kv_ref[pl.ds(wstart, WIN_KEYS), 2 * LANES:3 * LANES], qst)
    sw = sw + wb_ref[jnp.minimum(qb, WINDOW // qn)]

    ncp = kcvc_ref.shape[0]
    s = _dot(kcvc_ref[...], qst)
    n_i = lax.broadcasted_iota(jnp.int32, (ncp, 1), 0)
    p = _masked_softmax0(s, (n_i * CMP_STRIDE + (CMP_BLOCK - 1)) <= tq4)
    o_cmp = _dot(vct_ref[...], p.astype(BF16))
    psum = p[:, 0:qn]
    for r in range(1, rep):
        psum = psum + p[:, r * qn:(r + 1) * qn]
    imp = _dot_f32_rhs(ovt_ref[...], psum)

    ew = jnp.exp(sw - jnp.max(sw, axis=0, keepdims=True))
    lw = jnp.sum(ew, axis=0, keepdims=True)
    pw = ew.astype(BF16)

    j = lax.broadcasted_iota(jnp.int32, (n_sel, 1), 0)
    cur = tq // SEL_BLOCK
    forced = (j == 0) | (j == cur) | (j == cur - 1)
    score = jnp.where(forced, FORCE_VALUE, jnp.where(j * SEL_BLOCK <= tq, imp, MASK_VALUE))
    cnt = jnp.zeros((n_sel, qn), F32)
    for jp in range(n_sel):
        row = score[jp:jp + 1, :]
        earlier = jnp.where(j > jp, 1.0, 0.0)
        cnt = cnt + jnp.where(row > score, 1.0, jnp.where(row == score, earlier, 0.0))
    selbias = jnp.where(cnt < n_top, 0.0, MASK_VALUE)
    selbias = jnp.concatenate([selbias] * rep, axis=1)
    for cc in range(n_sel // blocks_per_chunk):
        sb_ref[cc] = selbias[cc * blocks_per_chunk:(cc + 1) * blocks_per_chunk, :]

    o_win = _dot(vwt_ref[c0], pw[0:qn, :])
    for cc in range(1, WIN_KEYS // qn):
        o_win = o_win + _dot(vwt_ref[c0 + cc], pw[cc * qn:(cc + 1) * qn, :])
    o_win = o_win / lw

    m_ref[...] = jnp.full(m_ref.shape, MASK_VALUE, F32)
    l_ref[...] = jnp.zeros(l_ref.shape, F32)
    acc_ref[...] = jnp.zeros(acc_ref.shape, F32)

    def sel_chunk(c, diagonal):
        start = pl.multiple_of(c * SEL_KC, SEL_KC)
        sc = _dot(kv_ref[pl.ds(start, SEL_KC), LANES:2 * LANES], qst)
        bias = sb_ref[c]
        sc = jnp.concatenate([sc[b * SEL_BLOCK:(b + 1) * SEL_BLOCK, :] + bias[b:b + 1, :]
                              for b in range(blocks_per_chunk)], axis=0)
        if diagonal:
            sc = sc + db_ref[qb % (SEL_KC // qn)]
        m_old = m_ref[...]
        m_new = jnp.maximum(m_old, jnp.max(sc, axis=0, keepdims=True))
        alpha = jnp.exp(m_old - m_new)
        pc = jnp.exp(sc - m_new)
        l_ref[...] = alpha * l_ref[...] + jnp.sum(pc, axis=0, keepdims=True)
        acc_ref[...] = alpha * acc_ref[...] + _dot(vst_ref[c], pc.astype(BF16))
        m_ref[...] = m_new

    last = (qb * qn) // SEL_KC

    def full_chunk(c, carry):
        sel_chunk(c, False)
        return carry

    lax.fori_loop(0, last, full_chunk, 0)
    sel_chunk(last, True)
    o_sel = acc_ref[...] / l_ref[...]

    gs = _sigmoid(gate_ref[...])
    out = None
    for r in range(rep):
        sl = slice(r * qn, (r + 1) * qn)
        o_r = (gs[3 * r:3 * r + 1, :] * o_cmp[:, sl] + gs[3 * r + 1:3 * r + 2, :] * o_sel[:, sl]
               + gs[3 * r + 2:3 * r + 3, :] * o_win[:, sl])
        term = _dot(o_r.T.astype(BF16), qout_ref[r])
        out = term if out is None else out + term
    o_ref[...] = out.astype(o_ref.dtype)


def _nsa(q, kv, kcvc, vct, gates_t, seq):
    t = q.shape[0]
    bsz = t // seq
    nqb = seq // Q_BLOCK
    n_sel = seq // SEL_BLOCK
    n_top = min(SEL_TOP, n_sel)
    ncp = seq // CMP_STRIDE
    assert seq >= WIN_KEYS and seq % SEL_KC == 0 and SEL_KC % Q_BLOCK == 0 and n_sel % 8 == 0
    pselt, qout, overlap_t, win_bias, diag_bias = _nsa_consts(seq)
    gw = NSA_REP * HEAD_DIM
    nq = NSA_REP * Q_BLOCK
    nck = seq // SEL_KC
    body = functools.partial(_nsa_body, n_top=n_top, n_sel=n_sel)
    return pl.pallas_call(
        body,
        grid=(bsz, NSA_GROUPS, nqb),
        in_specs=[pl.BlockSpec((Q_BLOCK, gw), lambda b, g, i: (b * nqb + i, g)),
                  pl.BlockSpec((seq, 3 * LANES), lambda b, g, i: (b, g)),
                  pl.BlockSpec((None, None, ncp, LANES), lambda b, g, i: (b, g, 0, 0)),
                  pl.BlockSpec((None, None, HEAD_DIM, ncp), lambda b, g, i: (b, g, 0, 0)),
                  pl.BlockSpec((None, None, 3 * NSA_REP, Q_BLOCK), lambda b, g, i: (b, g, 0, i)),
                  pl.BlockSpec(pselt.shape, lambda b, g, i: (0, 0, 0)),
                  pl.BlockSpec(qout.shape, lambda b, g, i: (0, 0, 0)),
                  pl.BlockSpec(overlap_t.shape, lambda b, g, i: (0, 0)),
                  pl.BlockSpec(win_bias.shape, lambda b, g, i: (0, 0, 0)),
                  pl.BlockSpec(diag_bias.shape, lambda b, g, i: (0, 0, 0))],
        out_specs=pl.BlockSpec((Q_BLOCK, gw), lambda b, g, i: (b * nqb + i, g)),
        out_shape=jax.ShapeDtypeStruct((t, NSA_HEADS * HEAD_DIM), BF16),
        scratch_shapes=[pltpu.VMEM((nck, HEAD_DIM, SEL_KC), BF16),
                        pltpu.VMEM((nqb, HEAD_DIM, Q_BLOCK), BF16),
                        pltpu.VMEM((nck, SEL_KC // SEL_BLOCK, nq), F32),
                        pltpu.VMEM((1, nq), F32),
                        pltpu.VMEM((1, nq), F32),
                        pltpu.VMEM((HEAD_DIM, nq), F32)],
        compiler_params=_params("arbitrary", "arbitrary", "arbitrary"),
        name="nsa",
    )(q, kv, kcvc, vct, gates_t, pselt, qout, overlap_t, win_bias, diag_bias)


def _dn_body(qkv_ref, sm_ref, z_ref, cw_ref, hp_ref, ng_ref, ones_ref, tri_ref, o_ref,
             state_ref, prev_ref):
    c = pl.program_id(1)
    ch, hd = DN_CHUNK, HEAD_DIM
    width = DN_HEADS * hd
    nb = qkv_ref.shape[0]
    npair = DN_HEADS // 2
    units = [(b, p) for b in range(nb) for p in range(npair)]

    @pl.when(c == 0)
    def _():
        state_ref[...] = jnp.zeros(state_ref.shape, F32)
        prev_ref[...] = jnp.zeros(prev_ref.shape, F32)

    lane = lax.broadcasted_iota(jnp.int32, (1, LANES), 1)
    first = lane < hd
    ri = lax.broadcasted_iota(jnp.int32, (2 * ch, 1), 0)
    ci = lax.broadcasted_iota(jnp.int32, (1, 2 * ch), 1)
    same = (ri // ch) == (ci // ch)
    causal = same & ((ri % ch) >= (ci % ch))
    strict = same & ((ri % ch) > (ci % ch))
    blockdiag = (lax.broadcasted_iota(jnp.int32, (LANES, 1), 0) // hd) == (lane // hd)
    ones_blk = ones_ref[...]

    def stack(v):
        return jnp.concatenate([jnp.where(first, v, 0.0), jnp.where(first, 0.0, v)], axis=0)

    def fold(v):
        return v[:ch] + v[ch:]

    def head_sumsq(v):
        return _dot((v * v).astype(BF16), ones_blk)

    ys, betas, gcs = [], [], []
    for b in range(nb):
        x = qkv_ref[b].astype(F32)
        xe = jnp.concatenate([prev_ref[b], x], axis=0)
        y = cw_ref[DN_CONV - 1:DN_CONV, :] * x
        for tap in range(DN_CONV - 1):
            y = y + cw_ref[tap:tap + 1, :] * pltpu.roll(xe, DN_CONV - 1 - tap, 0)[8:, :]
        prev_ref[b] = x[ch - 8:, :]
        ys.append(y * _sigmoid(y))
        sm = sm_ref[b]
        betas.append(_sigmoid(sm))
        xa = sm + hp_ref[1:2, :]
        softplus = jnp.maximum(xa, 0.0) + jnp.log(1.0 + jnp.exp(-jnp.abs(xa)))
        gcs.append(_dot_f32_rhs(tri_ref[...], -jnp.exp(hp_ref[0:1, :]) * softplus))

    def bc(tile, base, p):
        return jnp.where(first, tile[:, base + 2 * p:base + 2 * p + 1], tile[:, base + 2 * p + 1:base + 2 * p + 2])

    qn, kn, vb, kb, gc, egc, dec = {}, {}, {}, {}, {}, {}, {}
    for u in units:
        b, p = u
        y = ys[b]
        qp = y[:, p * LANES:(p + 1) * LANES]
        kp = y[:, width + p * LANES:width + (p + 1) * LANES]
        vp = y[:, 2 * width + p * LANES:2 * width + (p + 1) * LANES]
        qn[u] = qp * lax.rsqrt(head_sumsq(qp) + EPS) * (hd ** -0.5)
        kn[u] = kp * lax.rsqrt(head_sumsq(kp) + EPS)
        beta = bc(betas[b], SM_BETA, p)
        gc[u] = bc(gcs[b], SM_A, p)
        egc[u] = jnp.exp(gc[u])
        kb[u] = kn[u] * beta
        vb[u] = vp * beta
        gcol = jnp.concatenate([gcs[b][:, SM_A + 2 * p:SM_A + 2 * p + 1],
                                gcs[b][:, SM_A + 2 * p + 1:SM_A + 2 * p + 2]], axis=0)
        gmat = jnp.broadcast_to(gcol, (2 * ch, 2 * ch))
        dec[u] = jnp.where(causal, jnp.exp(jnp.where(causal, gmat - gmat.T, 0.0)), 0.0)

    nmat, attn, sol = {}, {}, {}
    for u in units:
        ks = jnp.concatenate([kn[u], kn[u]], axis=0).astype(BF16)
        nmat[u] = -(_dot_nt(stack(kb[u]).astype(BF16), ks) * jnp.where(strict, dec[u], 0.0))
        attn[u] = (_dot_nt(stack(qn[u]).astype(BF16), ks) * dec[u]).astype(BF16)
        sol[u] = jnp.concatenate([stack(vb[u]), stack(kb[u] * egc[u])], axis=1)

    n_fac = int(math.log2(ch))
    for it in range(n_fac):
        for u in units:
            nb16 = nmat[u].astype(BF16)
            sol[u] = sol[u] + _dot(nb16, sol[u].astype(BF16))
            if it < n_fac - 1:
                nmat[u] = _dot(nb16, nb16)

    v_new, st = {}, {}
    for u in units:
        b, p = u
        st[u] = state_ref[b, p]
        v_new[u] = fold(sol[u][:, :LANES]) - _dot(fold(sol[u][:, LANES:]).astype(BF16), st[u].astype(BF16))

    o = {}
    for u in units:
        b, p = u
        glast = gc[u][ch - 1:ch, :]
        o[u] = (_dot((qn[u] * egc[u]).astype(BF16), st[u].astype(BF16))
                + fold(_dot(attn[u], stack(v_new[u]).astype(BF16))))
        k_dec = kn[u] * jnp.exp(glast - gc[u])
        upd = _dot(k_dec.T.astype(BF16), v_new[u].astype(BF16))
        state_ref[b, p] = st[u] * jnp.exp(glast) + jnp.where(blockdiag, upd, 0.0)

    for u in units:
        b, p = u
        sl = slice(p * LANES, (p + 1) * LANES)
        on = o[u] * lax.rsqrt(head_sumsq(o[u]) * (1.0 / hd) + EPS) * ng_ref[...]
        zp = z_ref[b, :, sl].astype(F32)
        o_ref[b, :, sl] = (on * (zp * _sigmoid(zp))).astype(o_ref.dtype)


def _dn(qkv, sm, z, conv_w, hp, ng, seq):
    t = qkv.shape[0]
    bsz = t // seq
    nb = 2 if bsz % 2 == 0 else 1
    nc = seq // DN_CHUNK
    width = DN_HEADS * HEAD_DIM
    ones_blk = jnp.asarray(np.kron(np.eye(2), np.ones((HEAD_DIM, HEAD_DIM))), BF16)
    tri = jnp.asarray(np.tril(np.ones((DN_CHUNK, DN_CHUNK))), BF16)
    y = pl.pallas_call(
        _dn_body,
        grid=(bsz // nb, nc),
        in_specs=[pl.BlockSpec((nb, DN_CHUNK, 3 * width), lambda b, c: (b, c, 0)),
                  pl.BlockSpec((nb, DN_CHUNK, LANES), lambda b, c: (b, c, 0)),
                  pl.BlockSpec((nb, DN_CHUNK, width), lambda b, c: (b, c, 0)),
                  pl.BlockSpec(conv_w.shape, lambda b, c: (0, 0)),
                  pl.BlockSpec(hp.shape, lambda b, c: (0, 0)),
                  pl.BlockSpec(ng.shape, lambda b, c: (0, 0)),
                  pl.BlockSpec(ones_blk.shape, lambda b, c: (0, 0)),
                  pl.BlockSpec(tri.shape, lambda b, c: (0, 0))],
        out_specs=pl.BlockSpec((nb, DN_CHUNK, width), lambda b, c: (b, c, 0)),
        out_shape=jax.ShapeDtypeStruct((bsz, seq, width), BF16),
        scratch_shapes=[pltpu.VMEM((nb, DN_HEADS // 2, LANES, LANES), F32),
                        pltpu.VMEM((nb, 8, 3 * width), F32)],
        compiler_params=_params("arbitrary", "arbitrary"),
        name="dn",
    )(qkv.reshape(bsz, seq, 3 * width), sm.reshape(bsz, seq, LANES), z.reshape(bsz, seq, width),
      conv_w, hp, ng, ones_blk, tri)
    return y.reshape(t, width)


def _merge_body(yn_ref, yd_ref, mg_ref, x_ref, mod_ref, gf_ref, wb_ref, wo_ref, rwh_ref, rwl_ref,
                rb_ref, tri_ref, x1_ref, h_ref, rt_ref, cnt_ref):
    i = pl.program_id(0)
    d = x_ref.shape[1]

    @pl.when(i == 0)
    def _():
        cnt_ref[...] = jnp.zeros(cnt_ref.shape, F32)

    br0 = _dot(yn_ref[...], wb_ref[0])
    br1 = _dot(yd_ref[...], wb_ref[1])
    mixin = (_sigmoid(mg_ref[:, :d].astype(F32)) * br0 + _sigmoid(mg_ref[:, d:].astype(F32)) * br1)
    mix = _dot(mixin.astype(BF16), wo_ref[...])
    x1 = x_ref[...] + mod_ref[2:3, :] * mix
    x1_ref[...] = x1
    h = _rms_mod(x1, gf_ref[...], mod_ref[3:4, :], mod_ref[4:5, :])
    h_ref[...] = h

    hh = h.astype(BF16)
    hl = (h - hh.astype(F32)).astype(BF16)
    logits = _dot(hh, rwh_ref[...]) + _dot(hh, rwl_ref[...]) + _dot(hl, rwh_ref[...]) + rb_ref[...]
    lane = lax.broadcasted_iota(jnp.int32, (1, LANES), 1)
    cur = jnp.where(lane < N_EXPERTS, logits, PAD_SCORE)
    vals, idxs = [], []
    for _ in range(TOP_K):
        m = jnp.max(cur, axis=-1, keepdims=True)
        ix = jnp.min(jnp.where(cur == m, lane, LANES), axis=-1, keepdims=True)
        vals.append(m)
        idxs.append(ix)
        cur = jnp.where(lane == ix, PAD_SCORE, cur)
    es = [jnp.exp(v - vals[0]) for v in vals]
    den = es[0] + es[1] + es[2] + es[3]

    onehot = jnp.zeros(logits.shape, F32)
    for ix in idxs:
        onehot = onehot + jnp.where(lane == ix, 1.0, 0.0)
    before = _dot(tri_ref[...], onehot.astype(BF16)) + cnt_ref[...]
    cnt_ref[...] = cnt_ref[...] + jnp.sum(onehot, axis=0, keepdims=True)

    rt = jnp.zeros(logits.shape, F32)
    for k in range(TOP_K):
        rank = jnp.sum(jnp.where(lane == idxs[k], before, 0.0), axis=-1, keepdims=True)
        rt = jnp.where(lane == RT_IDX + k, idxs[k].astype(F32), rt)
        rt = jnp.where(lane == RT_W + k, es[k] / den, rt)
        rt = jnp.where(lane == RT_RANK + k, rank, rt)
    rt_ref[...] = rt


def _merge(y_nsa, y_dn, mg, x2, mod3, g_ffn, wb, wo, rwh, rwl, rb, seq):
    t, d = x2.shape
    tm = 512
    per_b = seq // tm
    hw = y_nsa.shape[1]
    tri = jnp.asarray(np.tril(np.ones((tm, tm)), -1), BF16)
    return pl.pallas_call(
        _merge_body,
        grid=(t // tm,),
        in_specs=[pl.BlockSpec((tm, hw), lambda i: (i, 0)),
                  pl.BlockSpec((tm, hw), lambda i: (i, 0)),
                  pl.BlockSpec((tm, 2 * d), lambda i: (i, 0)),
                  pl.BlockSpec((tm, d), lambda i: (i, 0)),
                  pl.BlockSpec((None, 6, d), lambda i: (i // per_b, 0, 0)),
                  pl.BlockSpec((1, d), lambda i: (0, 0)),
                  pl.BlockSpec(wb.shape, lambda i: (0, 0, 0)),
                  pl.BlockSpec(wo.shape, lambda i: (0, 0)),
                  pl.BlockSpec(rwh.shape, lambda i: (0, 0)),
                  pl.BlockSpec(rwl.shape, lambda i: (0, 0)),
                  pl.BlockSpec(rb.shape, lambda i: (0, 0)),
                  pl.BlockSpec(tri.shape, lambda i: (0, 0))],
        out_specs=[pl.BlockSpec((tm, d), lambda i: (i, 0)),
                   pl.BlockSpec((tm, d), lambda i: (i, 0)),
                   pl.BlockSpec((tm, LANES), lambda i: (i, 0)),
                   pl.BlockSpec((1, LANES), lambda i: (0, 0))],
        out_shape=[jax.ShapeDtypeStruct((t, d), F32),
                   jax.ShapeDtypeStruct((t, d), F32),
                   jax.ShapeDtypeStruct((t, LANES), F32),
                   jax.ShapeDtypeStruct((1, LANES), F32)],
        compiler_params=_params("arbitrary"),
        name="merge",
    )(y_nsa, y_dn, mg, x2, mod3, g_ffn.reshape(1, d), wb, wo, rwh, rwl, rb, tri)


def _row_copy(src, src_row, dst, dst_row, sem):
    return pltpu.make_async_copy(src.at[pl.ds(src_row, 1), :], dst.at[pl.ds(dst_row, 1), :], sem)


def _scatter_body(pad_lo_ref, pad_hi_ref, slot_ref, h_ref, xs_ref, zero_ref, sem, zsem):
    tm = h_ref.shape[0]

    @pl.when(pl.program_id(0) == 0)
    def _():
        zero_ref[...] = jnp.zeros(zero_ref.shape, F32)
        for e in range(N_EXPERTS):
            def zissue(r, carry):
                _row_copy(zero_ref, 0, xs_ref, r, zsem).start()
                return carry

            def zdrain(r, carry):
                _row_copy(zero_ref, 0, xs_ref, r, zsem).wait()
                return carry

            lax.fori_loop(pad_lo_ref[e], pad_hi_ref[e], zissue, 0)
            lax.fori_loop(pad_lo_ref[e], pad_hi_ref[e], zdrain, 0)

    def issue(r, carry):
        for k in range(TOP_K):
            _row_copy(h_ref, r, xs_ref, slot_ref[0, r * TOP_K + k], sem).start(priority=k % 2)
        return carry

    lax.fori_loop(0, tm, issue, 0)

    def drain(r, carry):
        for k in range(TOP_K):
            _row_copy(h_ref, r, xs_ref, slot_ref[0, r * TOP_K + k], sem).wait()
        return carry

    lax.fori_loop(0, tm, drain, 0)


def _scatter(pad_lo, pad_hi, slots3, h, n_rows):
    t, d = h.shape
    tm = SCATTER_TILE
    grid_spec = pltpu.PrefetchScalarGridSpec(
        num_scalar_prefetch=2,
        grid=(t // tm,),
        in_specs=[pl.BlockSpec((None, 1, tm * TOP_K), lambda i, lo, hi: (i, 0, 0), memory_space=pltpu.SMEM),
                  pl.BlockSpec((tm, d), lambda i, lo, hi: (i, 0))],
        out_specs=pl.BlockSpec(memory_space=pl.ANY),
        scratch_shapes=[pltpu.VMEM((8, d), F32), pltpu.SemaphoreType.DMA(()), pltpu.SemaphoreType.DMA(())],
    )
    return pl.pallas_call(
        _scatter_body,
        grid_spec=grid_spec,
        out_shape=jax.ShapeDtypeStruct((n_rows, d), F32),
        compiler_params=_params("arbitrary"),
        name="scatter",
    )(pad_lo, pad_hi, slots3, h)


def _experts_body(te_ref, nv_ref, xs_ref, w1_ref, b1_ref, w2_ref, b2_ref, ys_ref):
    i = pl.program_id(0)
    f = w2_ref.shape[0]

    @pl.when(i < nv_ref[0])
    def _():
        xb = xs_ref[...].astype(BF16)
        u = _dot(xb, w1_ref[...]) + b1_ref[...]
        x_glu = jnp.minimum(u[:, :f], SWIGLU_LIMIT)
        x_lin = jnp.clip(u[:, f:], -SWIGLU_LIMIT, SWIGLU_LIMIT)
        act = x_glu * _sigmoid(SWIGLU_ALPHA * x_glu) * (x_lin + 1.0)
        ys_ref[...] = _dot(act.astype(BF16), w2_ref[...]) + b2_ref[...]

    @pl.when(i >= nv_ref[0])
    def _():
        ys_ref[...] = jnp.zeros(ys_ref.shape, F32)


def _experts(tile_expert, n_valid, xs, w1, b1, w2, b2):
    p, d = xs.shape
    tm = EXPERT_TILE
    f = w2.shape[1]
    grid_spec = pltpu.PrefetchScalarGridSpec(
        num_scalar_prefetch=2,
        grid=(p // tm,),
        in_specs=[pl.BlockSpec((tm, d), lambda i, te, nv: (jnp.minimum(i, nv[0] - 1), 0)),
                  pl.BlockSpec((None, d, 2 * f), lambda i, te, nv: (te[i], 0, 0)),
                  pl.BlockSpec((None, 1, 2 * f), lambda i, te, nv: (te[i], 0, 0)),
                  pl.BlockSpec((None, f, d), lambda i, te, nv: (te[i], 0, 0)),
                  pl.BlockSpec((None, 1, d), lambda i, te, nv: (te[i], 0, 0))],
        out_specs=pl.BlockSpec((tm, d), lambda i, te, nv: (i, 0)),
    )
    return pl.pallas_call(
        _experts_body,
        grid_spec=grid_spec,
        out_shape=jax.ShapeDtypeStruct((p, d), F32),
        compiler_params=_params("arbitrary"),
        name="experts",
    )(tile_expert, n_valid, xs, w1, b1, w2, b2)


def _combine_body(slot_ref, rt_ref, x1_ref, mod_ref, g_ref, ys_ref, o_ref, buf_ref, sem):
    tm = x1_ref.shape[0]

    def issue(r, carry):
        for k in range(TOP_K):
            _row_copy(ys_ref, slot_ref[0, r * TOP_K + k], buf_ref.at[k], r, sem).start(priority=k % 2)
        return carry

    lax.fori_loop(0, tm, issue, 0)

    def drain(r, carry):
        for k in range(TOP_K):
            _row_copy(ys_ref, slot_ref[0, r * TOP_K + k], buf_ref.at[k], r, sem).wait()
        return carry

    lax.fori_loop(0, tm, drain, 0)

    rt = rt_ref[...]
    moe = rt[:, RT_W:RT_W + 1] * buf_ref[0]
    for k in range(1, TOP_K):
        moe = moe + rt[:, RT_W + k:RT_W + k + 1] * buf_ref[k]
    x2 = x1_ref[...] + mod_ref[5:6, :] * moe
    ms = jnp.mean(x2 * x2, axis=-1, keepdims=True)
    o_ref[...] = x2 * lax.rsqrt(ms + EPS) * g_ref[...]


def _combine(slots3, rt, x1, mod3, g_final, ys, seq):
    t, d = x1.shape
    tm = SCATTER_TILE
    per_b = seq // tm
    return pl.pallas_call(
        _combine_body,
        grid=(t // tm,),
        in_specs=[pl.BlockSpec((None, 1, tm * TOP_K), lambda i: (i, 0, 0), memory_space=pltpu.SMEM),
                  pl.BlockSpec((tm, LANES), lambda i: (i, 0)),
                  pl.BlockSpec((tm, d), lambda i: (i, 0)),
                  pl.BlockSpec((None, 6, d), lambda i: (i // per_b, 0, 0)),
                  pl.BlockSpec((1, d), lambda i: (0, 0)),
                  pl.BlockSpec(memory_space=pl.ANY)],
        out_specs=pl.BlockSpec((tm, d), lambda i: (i, 0)),
        out_shape=jax.ShapeDtypeStruct((t, d), F32),
        scratch_shapes=[pltpu.VMEM((TOP_K, tm, d), F32), pltpu.SemaphoreType.DMA(())],
        compiler_params=_params("arbitrary"),
        name="combine",
    )(slots3, rt, x1, mod3, g_final.reshape(1, d), ys)


def _pad_lanes(v, offset):
    out = jnp.zeros((1, LANES), F32)
    return out.at[0, offset:offset + v.shape[0]].set(v.astype(F32))


def kernel(x, c, w_ada, b_ada, g_norm_mix, w_in, cmp_pe_k, cmp_pe_v, cmp_w1, cmp_b1, cmp_w2,
           dn_conv_w, dn_a_log, dn_dt_bias, dn_norm_g, w_branch, w_out, g_norm_ffn,
           router_w, router_b, exp_w1, exp_b1, exp_w2, exp_b2, final_norm_g):
    bsz, seq, d = x.shape
    t = bsz * seq
    depth = w_ada.shape[0]
    assert depth == 1, "the final norm is fused into the last layer's combine step"
    x2 = x.reshape(t, d)
    cols = _in_columns()
    out = None
    for l in range(depth):
        mod3 = _ada(c, w_ada[l], b_ada[l]).reshape(bsz, 6, d)

        w_big = jnp.where(jnp.asarray(cols >= 0)[None, :], w_in[l][:, np.maximum(cols, 0)], 0.0).astype(BF16)
        q, kv, dnqkv, z, mg, sm = _inproj(x2, mod3, g_norm_mix[l], w_big, seq)

        nrow = seq // CMP_STRIDE
        src = kv.reshape(bsz, nrow, CMP_STRIDE, NSA_GROUPS, 6, HEAD_DIM)[:, :, :, :, 0:2, :]
        src = src.transpose(0, 3, 4, 1, 2, 5).reshape(bsz, NSA_GROUPS, 2, nrow, CMP_STRIDE * HEAD_DIM)
        pe = jnp.stack([cmp_pe_k[l], cmp_pe_v[l]]).reshape(2, 1, CMP_BLOCK * HEAD_DIM)
        pe = jnp.broadcast_to(pe, (2, 8, CMP_BLOCK * HEAD_DIM)).astype(BF16)
        w2p = jnp.zeros((2, CMP_HIDDEN, LANES), F32)
        w2p = w2p.at[0, :, :HEAD_DIM].set(cmp_w2[l, 0]).at[1, :, HEAD_DIM:].set(cmp_w2[l, 1]).astype(BF16)
        kcvc, vct = _cmp(src, cmp_w1[l].astype(BF16), pe, cmp_b1[l].reshape(2, 1, CMP_HIDDEN), w2p,
                         cmp_w2[l, 1].T.astype(BF16))

        gates_t = sm[:, SM_GATE:SM_GATE + 3 * NSA_HEADS].reshape(bsz, seq, NSA_GROUPS, 3 * NSA_REP)
        gates_t = gates_t.transpose(0, 2, 3, 1)
        y_nsa = _nsa(q, kv, kcvc, vct, gates_t, seq)

        hp = jnp.concatenate([_pad_lanes(dn_a_log[l], SM_A), _pad_lanes(dn_dt_bias[l], SM_A),
                              jnp.zeros((6, LANES), F32)], axis=0)
        ng = jnp.tile(dn_norm_g[l].reshape(1, HEAD_DIM), (1, 2))
        y_dn = _dn(dnqkv, sm, z, dn_conv_w[l], hp, ng, seq)

        rw = jnp.zeros((d, LANES), F32).at[:, :N_EXPERTS].set(router_w[l])
        rwh = rw.astype(BF16)
        rwl = (rw - rwh.astype(F32)).astype(BF16)
        x1, h, rt, cnt = _merge(y_nsa, y_dn, mg, x2, mod3, g_norm_ffn[l], w_branch[l].astype(BF16),
                                w_out[l].astype(BF16), rwh, rwl, _pad_lanes(router_b[l], 0), seq)

        counts = cnt[0, :N_EXPERTS].astype(jnp.int32)
        tiles_per = (counts + EXPERT_TILE - 1) // EXPERT_TILE
        tile_end = jnp.cumsum(tiles_per)
        offs = (tile_end - tiles_per) * EXPERT_TILE
        n_rows = t * TOP_K + N_EXPERTS * EXPERT_TILE
        n_tiles = n_rows // EXPERT_TILE
        idx = rt[:, RT_IDX:RT_IDX + TOP_K].astype(jnp.int32)
        rank = rt[:, RT_RANK:RT_RANK + TOP_K].astype(jnp.int32)
        slots = offs[idx] + rank
        slots3 = slots.reshape(t // SCATTER_TILE, 1, SCATTER_TILE * TOP_K)
        tile_ids = jnp.arange(n_tiles, dtype=jnp.int32)
        tile_expert = jnp.minimum(jnp.sum((tile_ids[:, None] >= tile_end[None, :]).astype(jnp.int32), axis=1),
                                  N_EXPERTS - 1).astype(jnp.int32)
        n_valid = tile_end[-1:].astype(jnp.int32)

        xs = _scatter(offs + counts, offs + tiles_per * EXPERT_TILE, slots3, h, n_rows)
        ys = _experts(tile_expert, n_valid, xs, exp_w1[l].astype(BF16),
                      exp_b1[l].reshape(N_EXPERTS, 1, -1), exp_w2[l].astype(BF16),
                      exp_b2[l].reshape(N_EXPERTS, 1, -1))
        out = _combine(slots3, rt, x1, mod3, final_norm_g, ys, seq)
    return out.reshape(bsz, seq, d)
```

```python
import functools
import math

import numpy as np
import jax
import jax.numpy as jnp
from jax import lax
from jax.experimental import pallas as pl
from jax.experimental.pallas import tpu as pltpu

F32 = jnp.float32
BF16 = jnp.bfloat16

HEAD_DIM = 64
NSA_HEADS = 8
NSA_GROUPS = 2
NSA_REP = NSA_HEADS // NSA_GROUPS
CMP_BLOCK = 32
CMP_STRIDE = 16
CMP_HIDDEN = 256
SEL_BLOCK = 64
SEL_TOP = 16
WINDOW = 512
Q_BLOCK = 256
DN_HEADS = 8
DN_CONV = 4
DN_CHUNK = 64
N_EXPERTS = 32
TOP_K = 4
SWIGLU_LIMIT = 7.0
SWIGLU_ALPHA = 1.702
EPS = 1e-6
MASK_VALUE = -1e30
FORCE_VALUE = 1e9
PAD_SCORE = -3e38

LANES = 128
VMEM_LIMIT = 56 * 2 ** 20

SEL_KC = 512
WIN_KEYS = WINDOW + Q_BLOCK
EXPERT_TILE = 512
SCATTER_TILE = 256

SM_GATE, SM_BETA, SM_A = 0, 24, 32
RT_IDX, RT_W, RT_RANK = 0, 4, 8


def _dot(a, b):
    return jnp.dot(a, b, preferred_element_type=F32)


def _dot_nt(a, b):
    return lax.dot_general(a, b, (((1,), (1,)), ((), ())), preferred_element_type=F32)


def _split3(x):
    hi = x.astype(BF16)
    r1 = x - hi.astype(F32)
    mid = r1.astype(BF16)
    lo = (r1 - mid.astype(F32)).astype(BF16)
    return hi, mid, lo


def _dot_f32_lhs(x, w_bf16):
    hi, mid, lo = _split3(x)
    return _dot(hi, w_bf16) + _dot(mid, w_bf16) + _dot(lo, w_bf16)


def _dot_f32_rhs(w_bf16, x):
    hi, mid, lo = _split3(x)
    return _dot(w_bf16, hi) + _dot(w_bf16, mid) + _dot(w_bf16, lo)


def _sigmoid(x):
    return 1.0 / (1.0 + jnp.exp(-x))


def _params(*sem):
    return pltpu.CompilerParams(dimension_semantics=sem, vmem_limit_bytes=VMEM_LIMIT)


def _ada_body(c_ref, w_ref, b_ref, o_ref):
    c = c_ref[...]
    a = (c * _sigmoid(c)).astype(BF16)
    o_ref[...] = _dot(a, w_ref[...].astype(BF16)) + b_ref[...]


def _ada(c, w, b):
    bsz, d = c.shape
    n = w.shape[1]
    tn = 1024
    return pl.pallas_call(
        _ada_body,
        grid=(n // tn,),
        in_specs=[pl.BlockSpec((bsz, d), lambda j: (0, 0)),
                  pl.BlockSpec((d, tn), lambda j: (0, j)),
                  pl.BlockSpec((1, tn), lambda j: (0, j))],
        out_specs=pl.BlockSpec((bsz, tn), lambda j: (0, j)),
        out_shape=jax.ShapeDtypeStruct((bsz, n), F32),
        compiler_params=_params("arbitrary"),
        name="ada",
    )(c, w, b.reshape(1, n))


IN_SEGS = (("q", 512, BF16), ("kv", 768, BF16), ("dn", 1536, BF16),
           ("z", 512, BF16), ("mg", 2048, BF16), ("sm", LANES, F32))


def _in_columns():
    q0 = 0
    kv0 = q0 + 512
    gate0 = kv0 + 768
    dn0 = gate0 + 24
    beta0 = dn0 + 1536
    a0 = beta0 + 8
    z0 = a0 + 8
    mg0 = z0 + 512
    cols = list(range(q0, q0 + 512))
    for g in range(NSA_GROUPS):
        for i in range(6):
            base = kv0 + i * NSA_GROUPS * HEAD_DIM + g * HEAD_DIM
            cols += list(range(base, base + HEAD_DIM))
    cols += list(range(dn0, dn0 + 1536))
    cols += list(range(z0, z0 + 512))
    cols += list(range(mg0, mg0 + 2048))
    small = list(range(gate0, gate0 + 24)) + list(range(beta0, beta0 + 8)) + list(range(a0, a0 + 8))
    cols += small + [-1] * (LANES - len(small))
    return np.asarray(cols, np.int32)


def _rms_mod(x, g, shift, scale):
    ms = jnp.mean(x * x, axis=-1, keepdims=True)
    y = x * lax.rsqrt(ms + EPS) * g
    return y * (1.0 + scale) + shift


def _inproj_body(x_ref, mod_ref, g_ref, w_ref, *out_refs):
    h = _rms_mod(x_ref[...], g_ref[...], mod_ref[0:1, :], mod_ref[1:2, :])
    hb = h.astype(BF16)
    off = 0
    for ref, (_, width, _) in zip(out_refs, IN_SEGS):
        for c0 in range(0, width, 512):
            cw = min(512, width - c0)
            ref[:, c0:c0 + cw] = _dot(hb, w_ref[:, off + c0:off + c0 + cw]).astype(ref.dtype)
        off += width


def _inproj(x2, mod3, g, w_big, seq):
    t, d = x2.shape
    tm = 512
    per_b = seq // tm
    nw = w_big.shape[1]
    return pl.pallas_call(
        _inproj_body,
        grid=(t // tm,),
        in_specs=[pl.BlockSpec((tm, d), lambda i: (i, 0)),
                  pl.BlockSpec((None, 6, d), lambda i: (i // per_b, 0, 0)),
                  pl.BlockSpec((1, d), lambda i: (0, 0)),
                  pl.BlockSpec((d, nw), lambda i: (0, 0))],
        out_specs=[pl.BlockSpec((tm, w), lambda i: (i, 0)) for _, w, _ in IN_SEGS],
        out_shape=[jax.ShapeDtypeStruct((t, w), dt) for _, w, dt in IN_SEGS],
        compiler_params=_params("arbitrary"),
        name="inproj",
    )(x2, mod3, g.reshape(1, d), w_big)


def _cmp_body(src_ref, w1_ref, pe_ref, b1_ref, w2_ref, w2t_ref, o_ref, ot_ref):
    half = CMP_STRIDE * HEAD_DIM
    out = None
    for kind in range(2):
        x = src_ref[kind]
        first = _dot(x, w1_ref[kind, :half, :])
        second = _dot(x, w1_ref[kind, half:, :])
        n = second.shape[0]
        second = pltpu.roll(second, n - 1, 0)
        pew = _dot(pe_ref[kind], w1_ref[kind])[0:1, :]
        pre = first + second + pew + b1_ref[kind]
        hid = (pre * _sigmoid(pre)).astype(BF16)
        term = _dot(hid, w2_ref[kind])
        out = term if out is None else out + term
    o_ref[...] = out.astype(o_ref.dtype)
    ot_ref[...] = _dot_nt(w2t_ref[...], hid).astype(ot_ref.dtype)


def _cmp(src, w1, pe, b1, w2p, w2t):
    bsz, ng, _, nrow, width = src.shape
    return pl.pallas_call(
        _cmp_body,
        grid=(bsz, ng),
        in_specs=[pl.BlockSpec((None, None, 2, nrow, width), lambda b, g: (b, g, 0, 0, 0)),
                  pl.BlockSpec(w1.shape, lambda b, g: (0, 0, 0)),
                  pl.BlockSpec(pe.shape, lambda b, g: (0, 0, 0)),
                  pl.BlockSpec(b1.shape, lambda b, g: (0, 0, 0)),
                  pl.BlockSpec(w2p.shape, lambda b, g: (0, 0, 0)),
                  pl.BlockSpec(w2t.shape, lambda b, g: (0, 0))],
        out_specs=[pl.BlockSpec((None, None, nrow, LANES), lambda b, g: (b, g, 0, 0)),
                   pl.BlockSpec((None, None, HEAD_DIM, nrow), lambda b, g: (b, g, 0, 0))],
        out_shape=[jax.ShapeDtypeStruct((bsz, ng, nrow, LANES), BF16),
                   jax.ShapeDtypeStruct((bsz, ng, HEAD_DIM, nrow), BF16)],
        compiler_params=_params("arbitrary", "arbitrary"),
        name="cmp",
    )(src, w1, pe, b1, w2p, w2t)


def _nsa_consts(seq):
    scale = HEAD_DIM ** -0.5
    gw = NSA_REP * HEAD_DIM
    pselt = np.zeros((NSA_REP, LANES, gw), np.float32)
    qout = np.zeros((NSA_REP, HEAD_DIM, gw), np.float32)
    for r in range(NSA_REP):
        for d in range(HEAD_DIM):
            pselt[r, d, r * HEAD_DIM + d] = scale
            qout[r, d, r * HEAD_DIM + d] = 1.0
    n_cmp = (seq - CMP_BLOCK) // CMP_STRIDE + 1
    n_sel = seq // SEL_BLOCK
    cs = np.arange(n_cmp)[:, None] * CMP_STRIDE
    ss = np.arange(n_sel)[None, :] * SEL_BLOCK
    ov = np.clip(np.minimum(cs + CMP_BLOCK, ss + SEL_BLOCK) - np.maximum(cs, ss), 0, None) / CMP_BLOCK
    overlap_t = np.zeros((n_sel, seq // CMP_STRIDE), np.float32)
    overlap_t[:, :n_cmp] = ov.T
    qrel = np.tile(np.arange(Q_BLOCK), NSA_REP)[None, None, :]
    shift = (np.arange(WINDOW // Q_BLOCK + 1) * Q_BLOCK)[:, None, None]
    delta = shift + qrel - np.arange(WIN_KEYS)[None, :, None]
    win_bias = np.where((delta >= 0) & (delta < WINDOW), 0.0, MASK_VALUE).astype(np.float32)
    shift = (np.arange(SEL_KC // Q_BLOCK) * Q_BLOCK)[:, None, None]
    diag_bias = np.where(np.arange(SEL_KC)[None, :, None] <= shift + qrel, 0.0, MASK_VALUE).astype(np.float32)
    return (jnp.asarray(pselt, BF16), jnp.asarray(qout, BF16), jnp.asarray(overlap_t, BF16),
            jnp.asarray(win_bias), jnp.asarray(diag_bias))


def _masked_softmax0(s, allowed):
    s = jnp.where(allowed, s, MASK_VALUE)
    e = jnp.exp(s - jnp.max(s, axis=0, keepdims=True))
    p = e / jnp.sum(e, axis=0, keepdims=True)
    return jnp.where(allowed, p, 0.0)


def _nsa_body(q_ref, kv_ref, kcvc_ref, vct_ref, gate_ref, pselt_ref, qout_ref, ovt_ref, wb_ref, db_ref,
              o_ref, vst_ref, vwt_ref, sb_ref, m_ref, l_ref, acc_ref, *, n_top, n_sel):
    qb = pl.program_id(2)
    rep, qn, hd = NSA_REP, Q_BLOCK, HEAD_DIM
    nq = rep * qn
    blocks_per_chunk = SEL_KC // SEL_BLOCK

    @pl.when(qb == 0)
    def _():
        for cc in range(vst_ref.shape[0]):
            blk = kv_ref[cc * SEL_KC:(cc + 1) * SEL_KC, LANES:2 * LANES].astype(F32)
            vst_ref[cc] = blk.T[hd:, :].astype(BF16)
        for cc in range(vwt_ref.shape[0]):
            blk = kv_ref[cc * qn:(cc + 1) * qn, 2 * LANES:3 * LANES].astype(F32)
            vwt_ref[cc] = blk.T[hd:, :].astype(BF16)

    q2 = q_ref[...]
    qst = jnp.concatenate([_dot_nt(pselt_ref[r], q2) for r in range(rep)], axis=1).astype(BF16)
    tq = qb * qn + lax.broadcasted_iota(jnp.int32, (1, qn), 1)
    tq4 = jnp.concatenate([tq] * rep, axis=1)

    c0 = jnp.maximum(qb - WINDOW // qn, 0)
    wstart = pl.multiple_of(c0 * qn, qn)
    sw = _dot(kv_ref[pl.ds(wstart, WIN_KEYS), 2 * LANES:3 * LANES], qst)
    sw = sw + wb_ref[jnp.minimum(qb, WINDOW // qn)]

    ncp = kcvc_ref.shape[0]
    s = _dot(kcvc_ref[...], qst)
    n_i = lax.broadcasted_iota(jnp.int32, (ncp, 1), 0)
    p = _masked_softmax0(s, (n_i * CMP_STRIDE + (CMP_BLOCK - 1)) <= tq4)
    o_cmp = _dot(vct_ref[...], p.astype(BF16))
    psum = p[:, 0:qn]
    for r in range(1, rep):
        psum = psum + p[:, r * qn:(r + 1) * qn]
    imp = _dot_f32_rhs(ovt_ref[...], psum)

    ew = jnp.exp(sw - jnp.max(sw, axis=0, keepdims=True))
    lw = jnp.sum(ew, axis=0, keepdims=True)
    pw = ew.astype(BF16)

    j = lax.broadcasted_iota(jnp.int32, (n_sel, 1), 0)
    cur = tq // SEL_BLOCK
    forced = (j == 0) | (j == cur) | (j == cur - 1)
    score = jnp.where(forced, FORCE_VALUE, jnp.where(j * SEL_BLOCK <= tq, imp, MASK_VALUE))
    cnt = jnp.zeros((n_sel, qn), F32)
    for jp in range(n_sel):
        row = score[jp:jp + 1, :]
        earlier = jnp.where(j > jp, 1.0, 0.0)
        cnt = cnt + jnp.where(row > score, 1.0, jnp.where(row == score, earlier, 0.0))
    selbias = jnp.where(cnt < n_top, 0.0, MASK_VALUE)
    selbias = jnp.concatenate([selbias] * rep, axis=1)
    for cc in range(n_sel // blocks_per_chunk):
        sb_ref[cc] = selbias[cc * blocks_per_chunk:(cc + 1) * blocks_per_chunk, :]

    o_win = _dot(vwt_ref[c0], pw[0:qn, :])
    for cc in range(1, WIN_KEYS // qn):
        o_win = o_win + _dot(vwt_ref[c0 + cc], pw[cc * qn:(cc + 1) * qn, :])
    o_win = o_win / lw

    m_ref[...] = jnp.full(m_ref.shape, MASK_VALUE, F32)
    l_ref[...] = jnp.zeros(l_ref.shape, F32)
    acc_ref[...] = jnp.zeros(acc_ref.shape, F32)

    def sel_chunk(c, diagonal):
        start = pl.multiple_of(c * SEL_KC, SEL_KC)
        sc = _dot(kv_ref[pl.ds(start, SEL_KC), LANES:2 * LANES], qst)
        bias = sb_ref[c]
        sc = jnp.concatenate([sc[b * SEL_BLOCK:(b + 1) * SEL_BLOCK, :] + bias[b:b + 1, :]
                              for b in range(blocks_per_chunk)], axis=0)
        if diagonal:
            sc = sc + db_ref[qb % (SEL_KC // qn)]
        m_old = m_ref[...]
        m_new = jnp.maximum(m_old, jnp.max(sc, axis=0, keepdims=True))
        alpha = jnp.exp(m_old - m_new)
        pc = jnp.exp(sc - m_new)
        l_ref[...] = alpha * l_ref[...] + jnp.sum(pc, axis=0, keepdims=True)
        acc_ref[...] = alpha * acc_ref[...] + _dot(vst_ref[c], pc.astype(BF16))
        m_ref[...] = m_new

    last = (qb * qn) // SEL_KC

    def full_chunk(c, carry):
        sel_chunk(c, False)
        return carry

    lax.fori_loop(0, last, full_chunk, 0)
    sel_chunk(last, True)
    o_sel = acc_ref[...] / l_ref[...]

    gs = _sigmoid(gate_ref[...])
    out = None
    for r in range(rep):
        sl = slice(r * qn, (r + 1) * qn)
        o_r = (gs[3 * r:3 * r + 1, :] * o_cmp[:, sl] + gs[3 * r + 1:3 * r + 2, :] * o_sel[:, sl]
               + gs[3 * r + 2:3 * r + 3, :] * o_win[:, sl])
        term = _dot(o_r.T.astype(BF16), qout_ref[r])
        out = term if out is None else out + term
    o_ref[...] = out.astype(o_ref.dtype)


def _nsa(q, kv, kcvc, vct, gates_t, seq):
    t = q.shape[0]
    bsz = t // seq
    nqb = seq // Q_BLOCK
    n_sel = seq // SEL_BLOCK
    n_top = min(SEL_TOP, n_sel)
    ncp = seq // CMP_STRIDE
    assert seq >= WIN_KEYS and seq % SEL_KC == 0 and SEL_KC % Q_BLOCK == 0 and n_sel % 8 == 0
    pselt, qout, overlap_t, win_bias, diag_bias = _nsa_consts(seq)
    gw = NSA_REP * HEAD_DIM
    nq = NSA_REP * Q_BLOCK
    nck = seq // SEL_KC
    body = functools.partial(_nsa_body, n_top=n_top, n_sel=n_sel)
    return pl.pallas_call(
        body,
        grid=(bsz, NSA_GROUPS, nqb),
        in_specs=[pl.BlockSpec((Q_BLOCK, gw), lambda b, g, i: (b * nqb + i, g)),
                  pl.BlockSpec((seq, 3 * LANES), lambda b, g, i: (b, g)),
                  pl.BlockSpec((None, None, ncp, LANES), lambda b, g, i: (b, g, 0, 0)),
                  pl.BlockSpec((None, None, HEAD_DIM, ncp), lambda b, g, i: (b, g, 0, 0)),
                  pl.BlockSpec((None, None, 3 * NSA_REP, Q_BLOCK), lambda b, g, i: (b, g, 0, i)),
                  pl.BlockSpec(pselt.shape, lambda b, g, i: (0, 0, 0)),
                  pl.BlockSpec(qout.shape, lambda b, g, i: (0, 0, 0)),
                  pl.BlockSpec(overlap_t.shape, lambda b, g, i: (0, 0)),
                  pl.BlockSpec(win_bias.shape, lambda b, g, i: (0, 0, 0)),
                  pl.BlockSpec(diag_bias.shape, lambda b, g, i: (0, 0, 0))],
        out_specs=pl.BlockSpec((Q_BLOCK, gw), lambda b, g, i: (b * nqb + i, g)),
        out_shape=jax.ShapeDtypeStruct((t, NSA_HEADS * HEAD_DIM), BF16),
        scratch_shapes=[pltpu.VMEM((nck, HEAD_DIM, SEL_KC), BF16),
                        pltpu.VMEM((nqb, HEAD_DIM, Q_BLOCK), BF16),
                        pltpu.VMEM((nck, SEL_KC // SEL_BLOCK, nq), F32),
                        pltpu.VMEM((1, nq), F32),
                        pltpu.VMEM((1, nq), F32),
                        pltpu.VMEM((HEAD_DIM, nq), F32)],
        compiler_params=_params("arbitrary", "arbitrary", "arbitrary"),
        name="nsa",
    )(q, kv, kcvc, vct, gates_t, pselt, qout, overlap_t, win_bias, diag_bias)


def _dn_body(qkv_ref, sm_ref, z_ref, cw_ref, hp_ref, ng_ref, ones_ref, tri_ref, o_ref,
             state_ref, prev_ref):
    c = pl.program_id(1)
    ch, hd = DN_CHUNK, HEAD_DIM
    width = DN_HEADS * hd
    nb = qkv_ref.shape[0]
    npair = DN_HEADS // 2
    units = [(b, p) for b in range(nb) for p in range(npair)]

    @pl.when(c == 0)
    def _():
        state_ref[...] = jnp.zeros(state_ref.shape, F32)
        prev_ref[...] = jnp.zeros(prev_ref.shape, F32)

    lane = lax.broadcasted_iota(jnp.int32, (1, LANES), 1)
    first = lane < hd
    ri = lax.broadcasted_iota(jnp.int32, (2 * ch, 1), 0)
    ci = lax.broadcasted_iota(jnp.int32, (1, 2 * ch), 1)
    same = (ri // ch) == (ci // ch)
    causal = same & ((ri % ch) >= (ci % ch))
    strict = same & ((ri % ch) > (ci % ch))
    blockdiag = (lax.broadcasted_iota(jnp.int32, (LANES, 1), 0) // hd) == (lane // hd)
    ones_blk = ones_ref[...]

    def stack(v):
        return jnp.concatenate([jnp.where(first, v, 0.0), jnp.where(first, 0.0, v)], axis=0)

    def fold(v):
        return v[:ch] + v[ch:]

    def head_sumsq(v):
        return _dot((v * v).astype(BF16), ones_blk)

    ys, betas, gcs = [], [], []
    for b in range(nb):
        x = qkv_ref[b].astype(F32)
        xe = jnp.concatenate([prev_ref[b], x], axis=0)
        y = cw_ref[DN_CONV - 1:DN_CONV, :] * x
        for tap in range(DN_CONV - 1):
            y = y + cw_ref[tap:tap + 1, :] * pltpu.roll(xe, DN_CONV - 1 - tap, 0)[8:, :]
        prev_ref[b] = x[ch - 8:, :]
        ys.append(y * _sigmoid(y))
        sm = sm_ref[b]
        betas.append(_sigmoid(sm))
        xa = sm + hp_ref[1:2, :]
        softplus = jnp.maximum(xa, 0.0) + jnp.log(1.0 + jnp.exp(-jnp.abs(xa)))
        gcs.append(_dot_f32_rhs(tri_ref[...], -jnp.exp(hp_ref[0:1, :]) * softplus))

    def bc(tile, base, p):
        return jnp.where(first, tile[:, base + 2 * p:base + 2 * p + 1], tile[:, base + 2 * p + 1:base + 2 * p + 2])

    qn, kn, vb, kb, gc, egc, dec = {}, {}, {}, {}, {}, {}, {}
    for u in units:
        b, p = u
        y = ys[b]
        qp = y[:, p * LANES:(p + 1) * LANES]
        kp = y[:, width + p * LANES:width + (p + 1) * LANES]
        vp = y[:, 2 * width + p * LANES:2 * width + (p + 1) * LANES]
        qn[u] = qp * lax.rsqrt(head_sumsq(qp) + EPS) * (hd ** -0.5)
        kn[u] = kp * lax.rsqrt(head_sumsq(kp) + EPS)
        beta = bc(betas[b], SM_BETA, p)
        gc[u] = bc(gcs[b], SM_A, p)
        egc[u] = jnp.exp(gc[u])
        kb[u] = kn[u] * beta
        vb[u] = vp * beta
        gcol = jnp.concatenate([gcs[b][:, SM_A + 2 * p:SM_A + 2 * p + 1],
                                gcs[b][:, SM_A + 2 * p + 1:SM_A + 2 * p + 2]], axis=0)
        gmat = jnp.broadcast_to(gcol, (2 * ch, 2 * ch))
        dec[u] = jnp.where(causal, jnp.exp(jnp.where(causal, gmat - gmat.T, 0.0)), 0.0)

    nmat, attn, sol = {}, {}, {}
    for u in units:
        ks = jnp.concatenate([kn[u], kn[u]], axis=0).astype(BF16)
        nmat[u] = -(_dot_nt(stack(kb[u]).astype(BF16), ks) * jnp.where(strict, dec[u], 0.0))
        attn[u] = (_dot_nt(stack(qn[u]).astype(BF16), ks) * dec[u]).astype(BF16)
        sol[u] = jnp.concatenate([stack(vb[u]), stack(kb[u] * egc[u])], axis=1)

    n_fac = int(math.log2(ch))
    for it in range(n_fac):
        for u in units:
            nb16 = nmat[u].astype(BF16)
            sol[u] = sol[u] + _dot(nb16, sol[u].astype(BF16))
            if it < n_fac - 1:
                nmat[u] = _dot(nb16, nb16)

    v_new, st = {}, {}
    for u in units:
        b, p = u
        st[u] = state_ref[b, p]
        v_new[u] = fold(sol[u][:, :LANES]) - _dot(fold(sol[u][:, LANES:]).astype(BF16), st[u].astype(BF16))

    o = {}
    for u in units:
        b, p = u
        glast = gc[u][ch - 1:ch, :]
        o[u] = (_dot((qn[u] * egc[u]).astype(BF16), st[u].astype(BF16))
                + fold(_dot(attn[u], stack(v_new[u]).astype(BF16))))
        k_dec = kn[u] * jnp.exp(glast - gc[u])
        upd = _dot(k_dec.T.astype(BF16), v_new[u].astype(BF16))
        state_ref[b, p] = st[u] * jnp.exp(glast) + jnp.where(blockdiag, upd, 0.0)

    for u in units:
        b, p = u
        sl = slice(p * LANES, (p + 1) * LANES)
        on = o[u] * lax.rsqrt(head_sumsq(o[u]) * (1.0 / hd) + EPS) * ng_ref[...]
        zp = z_ref[b, :, sl].astype(F32)
        o_ref[b, :, sl] = (on * (zp * _sigmoid(zp))).astype(o_ref.dtype)


def _dn(qkv, sm, z, conv_w, hp, ng, seq):
    t = qkv.shape[0]
    bsz = t // seq
    nb = 2 if bsz % 2 == 0 else 1
    nc = seq // DN_CHUNK
    width = DN_HEADS * HEAD_DIM
    ones_blk = jnp.asarray(np.kron(np.eye(2), np.ones((HEAD_DIM, HEAD_DIM))), BF16)
    tri = jnp.asarray(np.tril(np.ones((DN_CHUNK, DN_CHUNK))), BF16)
    y = pl.pallas_call(
        _dn_body,
        grid=(bsz // nb, nc),
        in_specs=[pl.BlockSpec((nb, DN_CHUNK, 3 * width), lambda b, c: (b, c, 0)),
                  pl.BlockSpec((nb, DN_CHUNK, LANES), lambda b, c: (b, c, 0)),
                  pl.BlockSpec((nb, DN_CHUNK, width), lambda b, c: (b, c, 0)),
                  pl.BlockSpec(conv_w.shape, lambda b, c: (0, 0)),
                  pl.BlockSpec(hp.shape, lambda b, c: (0, 0)),
                  pl.BlockSpec(ng.shape, lambda b, c: (0, 0)),
                  pl.BlockSpec(ones_blk.shape, lambda b, c: (0, 0)),
                  pl.BlockSpec(tri.shape, lambda b, c: (0, 0))],
        out_specs=pl.BlockSpec((nb, DN_CHUNK, width), lambda b, c: (b, c, 0)),
        out_shape=jax.ShapeDtypeStruct((bsz, seq, width), BF16),
        scratch_shapes=[pltpu.VMEM((nb, DN_HEADS // 2, LANES, LANES), F32),
                        pltpu.VMEM((nb, 8, 3 * width), F32)],
        compiler_params=_params("arbitrary", "arbitrary"),
        name="dn",
    )(qkv.reshape(bsz, seq, 3 * width), sm.reshape(bsz, seq, LANES), z.reshape(bsz, seq, width),
      conv_w, hp, ng, ones_blk, tri)
    return y.reshape(t, width)


def _merge_body(yn_ref, yd_ref, mg_ref, x_ref, mod_ref, gf_ref, wb_ref, wo_ref, rwh_ref, rwl_ref,
                rb_ref, tri_ref, x1_ref, h_ref, rt_ref, cnt_ref):
    i = pl.program_id(0)
    d = x_ref.shape[1]

    @pl.when(i == 0)
    def _():
        cnt_ref[...] = jnp.zeros(cnt_ref.shape, F32)

    br0 = _dot(yn_ref[...], wb_ref[0])
    br1 = _dot(yd_ref[...], wb_ref[1])
    mixin = (_sigmoid(mg_ref[:, :d].astype(F32)) * br0 + _sigmoid(mg_ref[:, d:].astype(F32)) * br1)
    mix = _dot(mixin.astype(BF16), wo_ref[...])
    x1 = x_ref[...] + mod_ref[2:3, :] * mix
    x1_ref[...] = x1
    h = _rms_mod(x1, gf_ref[...], mod_ref[3:4, :], mod_ref[4:5, :])
    h_ref[...] = h

    hh = h.astype(BF16)
    hl = (h - hh.astype(F32)).astype(BF16)
    logits = _dot(hh, rwh_ref[...]) + _dot(hh, rwl_ref[...]) + _dot(hl, rwh_ref[...]) + rb_ref[...]
    lane = lax.broadcasted_iota(jnp.int32, (1, LANES), 1)
    cur = jnp.where(lane < N_EXPERTS, logits, PAD_SCORE)
    vals, idxs = [], []
    for _ in range(TOP_K):
        m = jnp.max(cur, axis=-1, keepdims=True)
        ix = jnp.min(jnp.where(cur == m, lane, LANES), axis=-1, keepdims=True)
        vals.append(m)
        idxs.append(ix)
        cur = jnp.where(lane == ix, PAD_SCORE, cur)
    es = [jnp.exp(v - vals[0]) for v in vals]
    den = es[0] + es[1] + es[2] + es[3]

    onehot = jnp.zeros(logits.shape, F32)
    for ix in idxs:
        onehot = onehot + jnp.where(lane == ix, 1.0, 0.0)
    before = _dot(tri_ref[...], onehot.astype(BF16)) + cnt_ref[...]
    cnt_ref[...] = cnt_ref[...] + jnp.sum(onehot, axis=0, keepdims=True)

    rt = jnp.zeros(logits.shape, F32)
    for k in range(TOP_K):
        rank = jnp.sum(jnp.where(lane == idxs[k], before, 0.0), axis=-1, keepdims=True)
        rt = jnp.where(lane == RT_IDX + k, idxs[k].astype(F32), rt)
        rt = jnp.where(lane == RT_W + k, es[k] / den, rt)
        rt = jnp.where(lane == RT_RANK + k, rank, rt)
    rt_ref[...] = rt


def _merge(y_nsa, y_dn, mg, x2, mod3, g_ffn, wb, wo, rwh, rwl, rb, seq):
    t, d = x2.shape
    tm = 512
    per_b = seq // tm
    hw = y_nsa.shape[1]
    tri = jnp.asarray(np.tril(np.ones((tm, tm)), -1), BF16)
    return pl.pallas_call(
        _merge_body,
        grid=(t // tm,),
        in_specs=[pl.BlockSpec((tm, hw), lambda i: (i, 0)),
                  pl.BlockSpec((tm, hw), lambda i: (i, 0)),
                  pl.BlockSpec((tm, 2 * d), lambda i: (i, 0)),
                  pl.BlockSpec((tm, d), lambda i: (i, 0)),
                  pl.BlockSpec((None, 6, d), lambda i: (i // per_b, 0, 0)),
                  pl.BlockSpec((1, d), lambda i: (0, 0)),
                  pl.BlockSpec(wb.shape, lambda i: (0, 0, 0)),
                  pl.BlockSpec(wo.shape, lambda i: (0, 0)),
                  pl.BlockSpec(rwh.shape, lambda i: (0, 0)),
                  pl.BlockSpec(rwl.shape, lambda i: (0, 0)),
                  pl.BlockSpec(rb.shape, lambda i: (0, 0)),
                  pl.BlockSpec(tri.shape, lambda i: (0, 0))],
        out_specs=[pl.BlockSpec((tm, d), lambda i: (i, 0)),
                   pl.BlockSpec((tm, d), lambda i: (i, 0)),
                   pl.BlockSpec((tm, LANES), lambda i: (i, 0)),
                   pl.BlockSpec((1, LANES), lambda i: (0, 0))],
        out_shape=[jax.ShapeDtypeStruct((t, d), F32),
                   jax.ShapeDtypeStruct((t, d), F32),
                   jax.ShapeDtypeStruct((t, LANES), F32),
                   jax.ShapeDtypeStruct((1, LANES), F32)],
        compiler_params=_params("arbitrary"),
        name="merge",
    )(y_nsa, y_dn, mg, x2, mod3, g_ffn.reshape(1, d), wb, wo, rwh, rwl, rb, tri)


def _row_copy(src, src_row, dst, dst_row, sem):
    return pltpu.make_async_copy(src.at[pl.ds(src_row, 1), :], dst.at[pl.ds(dst_row, 1), :], sem)


def _scatter_body(pad_lo_ref, pad_hi_ref, slot_ref, h_ref, xs_ref, zero_ref, sem, zsem):
    tm = h_ref.shape[0]

    @pl.when(pl.program_id(0) == 0)
    def _():
        zero_ref[...] = jnp.zeros(zero_ref.shape, F32)
        for e in range(N_EXPERTS):
            def zissue(r, carry):
                _row_copy(zero_ref, 0, xs_ref, r, zsem).start()
                return carry

            def zdrain(r, carry):
                _row_copy(zero_ref, 0, xs_ref, r, zsem).wait()
                return carry

            lax.fori_loop(pad_lo_ref[e], pad_hi_ref[e], zissue, 0)
            lax.fori_loop(pad_lo_ref[e], pad_hi_ref[e], zdrain, 0)

    def issue(r, carry):
        for k in range(TOP_K):
            _row_copy(h_ref, r, xs_ref, slot_ref[0, r * TOP_K + k], sem).start(priority=k % 2)
        return carry

    lax.fori_loop(0, tm, issue, 0)

    def drain(r, carry):
        for k in range(TOP_K):
            _row_copy(h_ref, r, xs_ref, slot_ref[0, r * TOP_K + k], sem).wait()
        return carry

    lax.fori_loop(0, tm, drain, 0)


def _scatter(pad_lo, pad_hi, slots3, h, n_rows):
    t, d = h.shape
    tm = SCATTER_TILE
    grid_spec = pltpu.PrefetchScalarGridSpec(
        num_scalar_prefetch=2,
        grid=(t // tm,),
        in_specs=[pl.BlockSpec((None, 1, tm * TOP_K), lambda i, lo, hi: (i, 0, 0), memory_space=pltpu.SMEM),
                  pl.BlockSpec((tm, d), lambda i, lo, hi: (i, 0))],
        out_specs=pl.BlockSpec(memory_space=pl.ANY),
        scratch_shapes=[pltpu.VMEM((8, d), F32), pltpu.SemaphoreType.DMA(()), pltpu.SemaphoreType.DMA(())],
    )
    return pl.pallas_call(
        _scatter_body,
        grid_spec=grid_spec,
        out_shape=jax.ShapeDtypeStruct((n_rows, d), F32),
        compiler_params=_params("arbitrary"),
        name="scatter",
    )(pad_lo, pad_hi, slots3, h)


def _experts_body(te_ref, nv_ref, xs_ref, w1_ref, b1_ref, w2_ref, b2_ref, ys_ref):
    i = pl.program_id(0)
    f = w2_ref.shape[0]

    @pl.when(i < nv_ref[0])
    def _():
        xb = xs_ref[...].astype(BF16)
        u = _dot(xb, w1_ref[...]) + b1_ref[...]
        x_glu = jnp.minimum(u[:, :f], SWIGLU_LIMIT)
        x_lin = jnp.clip(u[:, f:], -SWIGLU_LIMIT, SWIGLU_LIMIT)
        act = x_glu * _sigmoid(SWIGLU_ALPHA * x_glu) * (x_lin + 1.0)
        ys_ref[...] = _dot(act.astype(BF16), w2_ref[...]) + b2_ref[...]

    @pl.when(i >= nv_ref[0])
    def _():
        ys_ref[...] = jnp.zeros(ys_ref.shape, F32)


def _experts(tile_expert, n_valid, xs, w1, b1, w2, b2):
    p, d = xs.shape
    tm = EXPERT_TILE
    f = w2.shape[1]
    grid_spec = pltpu.PrefetchScalarGridSpec(
        num_scalar_prefetch=2,
        grid=(p // tm,),
        in_specs=[pl.BlockSpec((tm, d), lambda i, te, nv: (jnp.minimum(i, nv[0] - 1), 0)),
                  pl.BlockSpec((None, d, 2 * f), lambda i, te, nv: (te[i], 0, 0)),
                  pl.BlockSpec((None, 1, 2 * f), lambda i, te, nv: (te[i], 0, 0)),
                  pl.BlockSpec((None, f, d), lambda i, te, nv: (te[i], 0, 0)),
                  pl.BlockSpec((None, 1, d), lambda i, te, nv: (te[i], 0, 0))],
        out_specs=pl.BlockSpec((tm, d), lambda i, te, nv: (i, 0)),
    )
    return pl.pallas_call(
        _experts_body,
        grid_spec=grid_spec,
        out_shape=jax.ShapeDtypeStruct((p, d), F32),
        compiler_params=_params("arbitrary"),
        name="experts",
    )(tile_expert, n_valid, xs, w1, b1, w2, b2)


def _combine_body(slot_ref, rt_ref, x1_ref, mod_ref, g_ref, ys_ref, o_ref, buf_ref, sem):
    tm = x1_ref.shape[0]

    def issue(r, carry):
        for k in range(TOP_K):
            _row_copy(ys_ref, slot_ref[0, r * TOP_K + k], buf_ref.at[k], r, sem).start(priority=k % 2)
        return carry

    lax.fori_loop(0, tm, issue, 0)

    def drain(r, carry):
        for k in range(TOP_K):
            _row_copy(ys_ref, slot_ref[0, r * TOP_K + k], buf_ref.at[k], r, sem).wait()
        return carry

    lax.fori_loop(0, tm, drain, 0)

    rt = rt_ref[...]
    moe = rt[:, RT_W:RT_W + 1] * buf_ref[0]
    for k in range(1, TOP_K):
        moe = moe + rt[:, RT_W + k:RT_W + k + 1] * buf_ref[k]
    x2 = x1_ref[...] + mod_ref[5:6, :] * moe
    ms = jnp.mean(x2 * x2, axis=-1, keepdims=True)
    o_ref[...] = x2 * lax.rsqrt(ms + EPS) * g_ref[...]


def _combine(slots3, rt, x1, mod3, g_final, ys, seq):
    t, d = x1.shape
    tm = SCATTER_TILE
    per_b = seq // tm
    return pl.pallas_call(
        _combine_body,
        grid=(t // tm,),
        in_specs=[pl.BlockSpec((None, 1, tm * TOP_K), lambda i: (i, 0, 0), memory_space=pltpu.SMEM),
                  pl.BlockSpec((tm, LANES), lambda i: (i, 0)),
                  pl.BlockSpec((tm, d), lambda i: (i, 0)),
                  pl.BlockSpec((None, 6, d), lambda i: (i // per_b, 0, 0)),
                  pl.BlockSpec((1, d), lambda i: (0, 0)),
                  pl.BlockSpec(memory_space=pl.ANY)],
        out_specs=pl.BlockSpec((tm, d), lambda i: (i, 0)),
        out_shape=jax.ShapeDtypeStruct((t, d), F32),
        scratch_shapes=[pltpu.VMEM((TOP_K, tm, d), F32), pltpu.SemaphoreType.DMA(())],
        compiler_params=_params("arbitrary"),
        name="combine",
    )(slots3, rt, x1, mod3, g_final.reshape(1, d), ys)


def _pad_lanes(v, offset):
    out = jnp.zeros((1, LANES), F32)
    return out.at[0, offset:offset + v.shape[0]].set(v.astype(F32))


def kernel(x, c, w_ada, b_ada, g_norm_mix, w_in, cmp_pe_k, cmp_pe_v, cmp_w1, cmp_b1, cmp_w2,
           dn_conv_w, dn_a_log, dn_dt_bias, dn_norm_g, w_branch, w_out, g_norm_ffn,
           router_w, router_b, exp_w1, exp_b1, exp_w2, exp_b2, final_norm_g):
    bsz, seq, d = x.shape
    t = bsz * seq
    depth = w_ada.shape[0]
    assert depth == 1, "the final norm is fused into the last layer's combine step"
    x2 = x.reshape(t, d)
    cols = _in_columns()
    out = None
    for l in range(depth):
        mod3 = _ada(c, w_ada[l], b_ada[l]).reshape(bsz, 6, d)

        w_big = jnp.where(jnp.asarray(cols >= 0)[None, :], w_in[l][:, np.maximum(cols, 0)], 0.0).astype(BF16)
        q, kv, dnqkv, z, mg, sm = _inproj(x2, mod3, g_norm_mix[l], w_big, seq)

        nrow = seq // CMP_STRIDE
        src = kv.reshape(bsz, nrow, CMP_STRIDE, NSA_GROUPS, 6, HEAD_DIM)[:, :, :, :, 0:2, :]
        src = src.transpose(0, 3, 4, 1, 2, 5).reshape(bsz, NSA_GROUPS, 2, nrow, CMP_STRIDE * HEAD_DIM)
        pe = jnp.stack([cmp_pe_k[l], cmp_pe_v[l]]).reshape(2, 1, CMP_BLOCK * HEAD_DIM)
        pe = jnp.broadcast_to(pe, (2, 8, CMP_BLOCK * HEAD_DIM)).astype(BF16)
        w2p = jnp.zeros((2, CMP_HIDDEN, LANES), F32)
        w2p = w2p.at[0, :, :HEAD_DIM].set(cmp_w2[l, 0]).at[1, :, HEAD_DIM:].set(cmp_w2[l, 1]).astype(BF16)
        kcvc, vct = _cmp(src, cmp_w1[l].astype(BF16), pe, cmp_b1[l].reshape(2, 1, CMP_HIDDEN), w2p,
                         cmp_w2[l, 1].T.astype(BF16))

        gates_t = sm[:, SM_GATE:SM_GATE + 3 * NSA_HEADS].reshape(bsz, seq, NSA_GROUPS, 3 * NSA_REP)
        gates_t = gates_t.transpose(0, 2, 3, 1)
        y_nsa = _nsa(q, kv, kcvc, vct, gates_t, seq)

        hp = jnp.concatenate([_pad_lanes(dn_a_log[l], SM_A), _pad_lanes(dn_dt_bias[l], SM_A),
                              jnp.zeros((6, LANES), F32)], axis=0)
        ng = jnp.tile(dn_norm_g[l].reshape(1, HEAD_DIM), (1, 2))
        y_dn = _dn(dnqkv, sm, z, dn_conv_w[l], hp, ng, seq)

        rw = jnp.zeros((d, LANES), F32).at[:, :N_EXPERTS].set(router_w[l])
        rwh = rw.astype(BF16)
        rwl = (rw - rwh.astype(F32)).astype(BF16)
        x1, h, rt, cnt = _merge(y_nsa, y_dn, mg, x2, mod3, g_norm_ffn[l], w_branch[l].astype(BF16),
                                w_out[l].astype(BF16), rwh, rwl, _pad_lanes(router_b[l], 0), seq)

        counts = cnt[0, :N_EXPERTS].astype(jnp.int32)
        tiles_per = (counts + EXPERT_TILE - 1) // EXPERT_TILE
        tile_end = jnp.cumsum(tiles_per)
        offs = (tile_end - tiles_per) * EXPERT_TILE
        n_rows = t * TOP_K + N_EXPERTS * EXPERT_TILE
        n_tiles = n_rows // EXPERT_TILE
        idx = rt[:, RT_IDX:RT_IDX + TOP_K].astype(jnp.int32)
        rank = rt[:, RT_RANK:RT_RANK + TOP_K].astype(jnp.int32)
        slots = offs[idx] + rank
        slots3 = slots.reshape(t // SCATTER_TILE, 1, SCATTER_TILE * TOP_K)
        tile_ids = jnp.arange(n_tiles, dtype=jnp.int32)
        tile_expert = jnp.minimum(jnp.sum((tile_ids[:, None] >= tile_end[None, :]).astype(jnp.int32), axis=1),
                                  N_EXPERTS - 1).astype(jnp.int32)
        n_valid = tile_end[-1:].astype(jnp.int32)

        xs = _scatter(offs + counts, offs + tiles_per * EXPERT_TILE, slots3, h, n_rows)
        ys = _experts(tile_expert, n_valid, xs, exp_w1[l].astype(BF16),
                      exp_b1[l].reshape(N_EXPERTS, 1, -1), exp_w2[l].astype(BF16),
                      exp_b2[l].reshape(N_EXPERTS, 1, -1))
        out = _combine(slots3, rt, x1, mod3, final_norm_g, ys, seq)
    return out.reshape(bsz, seq, d)
```

```python
import functools

import numpy as np
import jax
import jax.numpy as jnp
from jax import lax
from jax.experimental import pallas as pl
from jax.experimental.pallas import tpu as pltpu

F32 = jnp.float32
BF16 = jnp.bfloat16

HEAD_DIM = 64
NSA_HEADS = 8
NSA_GROUPS = 2
NSA_REP = NSA_HEADS // NSA_GROUPS
CMP_BLOCK = 32
CMP_STRIDE = 16
CMP_HIDDEN = 256
SEL_BLOCK = 64
SEL_TOP = 16
WINDOW = 512
Q_BLOCK = 256
DN_HEADS = 8
DN_CONV = 4
DN_CHUNK = 64
N_EXPERTS = 32
TOP_K = 4
SWIGLU_LIMIT = 7.0
SWIGLU_ALPHA = 1.702
EPS = 1e-6
MASK_VALUE = -1e30
FORCE_VALUE = 1e9
PAD_SCORE = -3e38

LANES = 128
VMEM_LIMIT = 56 * 2 ** 20

SEL_KC = 512
WIN_KEYS = WINDOW + Q_BLOCK
EXPERT_TILE = 512
SCATTER_TILE = 256

SM_GATE, SM_BETA, SM_A = 0, 24, 32
RT_IDX, RT_W, RT_RANK = 0, 4, 8


def _dot(a, b):
    return jnp.dot(a, b, preferred_element_type=F32)


def _dot_nt(a, b):
    return lax.dot_general(a, b, (((1,), (1,)), ((), ())), preferred_element_type=F32)


def _split3(x):
    hi = x.astype(BF16)
    r1 = x - hi.astype(F32)
    mid = r1.astype(BF16)
    lo = (r1 - mid.astype(F32)).astype(BF16)
    return hi, mid, lo


def _dot_f32_lhs(x, w_bf16):
    hi, mid, lo = _split3(x)
    return _dot(hi, w_bf16) + _dot(mid, w_bf16) + _dot(lo, w_bf16)


def _dot_f32_rhs(w_bf16, x):
    hi, mid, lo = _split3(x)
    return _dot(w_bf16, hi) + _dot(w_bf16, mid) + _dot(w_bf16, lo)


def _sigmoid(x):
    return 1.0 / (1.0 + jnp.exp(-x))


def _params(*sem):
    return pltpu.CompilerParams(dimension_semantics=sem, vmem_limit_bytes=VMEM_LIMIT)


def _ada_body(c_ref, w_ref, b_ref, o_ref):
    c = c_ref[...]
    a = (c * _sigmoid(c)).astype(BF16)
    o_ref[...] = _dot(a, w_ref[...].astype(BF16)) + b_ref[...]


def _ada(c, w, b):
    bsz, d = c.shape
    n = w.shape[1]
    tn = 1024
    return pl.pallas_call(
        _ada_body,
        grid=(n // tn,),
        in_specs=[pl.BlockSpec((bsz, d), lambda j: (0, 0)),
                  pl.BlockSpec((d, tn), lambda j: (0, j)),
                  pl.BlockSpec((1, tn), lambda j: (0, j))],
        out_specs=pl.BlockSpec((bsz, tn), lambda j: (0, j)),
        out_shape=jax.ShapeDtypeStruct((bsz, n), F32),
        compiler_params=_params("arbitrary"),
        name="ada",
    )(c, w, b.reshape(1, n))


IN_SEGS = (("q", 512, BF16), ("kv", 768, BF16), ("dn", 1536, BF16),
           ("z", 512, BF16), ("mg", 2048, BF16), ("sm", LANES, F32))


def _in_columns():
    q0 = 0
    kv0 = q0 + 512
    gate0 = kv0 + 768
    dn0 = gate0 + 24
    beta0 = dn0 + 1536
    a0 = beta0 + 8
    z0 = a0 + 8
    mg0 = z0 + 512
    cols = list(range(q0, q0 + 512))
    for g in range(NSA_GROUPS):
        for i in range(6):
            base = kv0 + i * NSA_GROUPS * HEAD_DIM + g * HEAD_DIM
            cols += list(range(base, base + HEAD_DIM))
    cols += list(range(dn0, dn0 + 1536))
    cols += list(range(z0, z0 + 512))
    cols += list(range(mg0, mg0 + 2048))
    small = list(range(gate0, gate0 + 24)) + list(range(beta0, beta0 + 8)) + list(range(a0, a0 + 8))
    cols += small + [-1] * (LANES - len(small))
    return np.asarray(cols, np.int32)


def _rms_mod(x, g, shift, scale):
    ms = jnp.mean(x * x, axis=-1, keepdims=True)
    y = x * lax.rsqrt(ms + EPS) * g
    return y * (1.0 + scale) + shift


def _inproj_body(x_ref, mod_ref, g_ref, w_ref, *out_refs):
    h = _rms_mod(x_ref[...], g_ref[...], mod_ref[0:1, :], mod_ref[1:2, :])
    hb = h.astype(BF16)
    off = 0
    for ref, (_, width, _) in zip(out_refs, IN_SEGS):
        for c0 in range(0, width, 512):
            cw = min(512, width - c0)
            ref[:, c0:c0 + cw] = _dot(hb, w_ref[:, off + c0:off + c0 + cw]).astype(ref.dtype)
        off += width


def _inproj(x2, mod3, g, w_big, seq):
    t, d = x2.shape
    tm = 512
    per_b = seq // tm
    nw = w_big.shape[1]
    return pl.pallas_call(
        _inproj_body,
        grid=(t // tm,),
        in_specs=[pl.BlockSpec((tm, d), lambda i: (i, 0)),
                  pl.BlockSpec((None, 6, d), lambda i: (i // per_b, 0, 0)),
                  pl.BlockSpec((1, d), lambda i: (0, 0)),
                  pl.BlockSpec((d, nw), lambda i: (0, 0))],
        out_specs=[pl.BlockSpec((tm, w), lambda i: (i, 0)) for _, w, _ in IN_SEGS],
        out_shape=[jax.ShapeDtypeStruct((t, w), dt) for _, w, dt in IN_SEGS],
        compiler_params=_params("arbitrary"),
        name="inproj",
    )(x2, mod3, g.reshape(1, d), w_big)


def _cmp_body(src_ref, w1_ref, pe_ref, b1_ref, w2_ref, w2t_ref, o_ref, ot_ref):
    half = CMP_STRIDE * HEAD_DIM
    out = None
    for kind in range(2):
        x = src_ref[kind]
        first = _dot(x, w1_ref[kind, :half, :])
        second = _dot(x, w1_ref[kind, half:, :])
        n = second.shape[0]
        second = pltpu.roll(second, n - 1, 0)
        pew = _dot(pe_ref[kind], w1_ref[kind])[0:1, :]
        pre = first + second + pew + b1_ref[kind]
        hid = (pre * _sigmoid(pre)).astype(BF16)
        term = _dot(hid, w2_ref[kind])
        out = term if out is None else out + term
    o_ref[...] = out.astype(o_ref.dtype)
    ot_ref[...] = _dot_nt(w2t_ref[...], hid).astype(ot_ref.dtype)


def _cmp(src, w1, pe, b1, w2p, w2t):
    bsz, ng, _, nrow, width = src.shape
    return pl.pallas_call(
        _cmp_body,
        grid=(bsz, ng),
        in_specs=[pl.BlockSpec((None, None, 2, nrow, width), lambda b, g: (b, g, 0, 0, 0)),
                  pl.BlockSpec(w1.shape, lambda b, g: (0, 0, 0)),
                  pl.BlockSpec(pe.shape, lambda b, g: (0, 0, 0)),
                  pl.BlockSpec(b1.shape, lambda b, g: (0, 0, 0)),
                  pl.BlockSpec(w2p.shape, lambda b, g: (0, 0, 0)),
                  pl.BlockSpec(w2t.shape, lambda b, g: (0, 0))],
        out_specs=[pl.BlockSpec((None, None, nrow, LANES), lambda b, g: (b, g, 0, 0)),
                   pl.BlockSpec((None, None, HEAD_DIM, nrow), lambda b, g: (b, g, 0, 0))],
        out_shape=[jax.ShapeDtypeStruct((bsz, ng, nrow, LANES), BF16),
                   jax.ShapeDtypeStruct((bsz, ng, HEAD_DIM, nrow), BF16)],
        compiler_params=_params("arbitrary", "arbitrary"),
        name="cmp",
    )(src, w1, pe, b1, w2p, w2t)


def _nsa_consts(seq):
    scale = HEAD_DIM ** -0.5
    gw = NSA_REP * HEAD_DIM
    pselt = np.zeros((NSA_REP, LANES, gw), np.float32)
    qout = np.zeros((NSA_REP, HEAD_DIM, gw), np.float32)
    for r in range(NSA_REP):
        for d in range(HEAD_DIM):
            pselt[r, d, r * HEAD_DIM + d] = scale
            qout[r, d, r * HEAD_DIM + d] = 1.0
    n_cmp = (seq - CMP_BLOCK) // CMP_STRIDE + 1
    n_sel = seq // SEL_BLOCK
    cs = np.arange(n_cmp)[:, None] * CMP_STRIDE
    ss = np.arange(n_sel)[None, :] * SEL_BLOCK
    ov = np.clip(np.minimum(cs + CMP_BLOCK, ss + SEL_BLOCK) - np.maximum(cs, ss), 0, None) / CMP_BLOCK
    overlap_t = np.zeros((n_sel, seq // CMP_STRIDE), np.float32)
    overlap_t[:, :n_cmp] = ov.T
    qrel = np.tile(np.arange(Q_BLOCK), NSA_REP)[None, None, :]
    shift = (np.arange(WINDOW // Q_BLOCK + 1) * Q_BLOCK)[:, None, None]
    delta = shift + qrel - np.arange(WIN_KEYS)[None, :, None]
    win_bias = np.where((delta >= 0) & (delta < WINDOW), 0.0, MASK_VALUE).astype(np.float32)
    shift = (np.arange(SEL_KC // Q_BLOCK) * Q_BLOCK)[:, None, None]
    diag_bias = np.where(np.arange(SEL_KC)[None, :, None] <= shift + qrel, 0.0, MASK_VALUE).astype(np.float32)
    return (jnp.asarray(pselt, BF16), jnp.asarray(qout, BF16), jnp.asarray(overlap_t, BF16),
            jnp.asarray(win_bias), jnp.asarray(diag_bias))


def _masked_softmax0(s, allowed):
    s = jnp.where(allowed, s, MASK_VALUE)
    e = jnp.exp(s - jnp.max(s, axis=0, keepdims=True))
    p = e / jnp.sum(e, axis=0, keepdims=True)
    return jnp.where(allowed, p, 0.0)


def _nsa_body(q_ref, kv_ref, kcvc_ref, vct_ref, gate_ref, pselt_ref, qout_ref, ovt_ref, wb_ref, db_ref,
              o_ref, vst_ref, vwt_ref, sb_ref, m_ref, l_ref, acc_ref, *, n_top, n_sel):
    qb = pl.program_id(2)
    rep, qn, hd = NSA_REP, Q_BLOCK, HEAD_DIM
    nq = rep * qn
    blocks_per_chunk = SEL_KC // SEL_BLOCK

    @pl.when(qb == 0)
    def _():
        for cc in range(vst_ref.shape[0]):
            blk = kv_ref[cc * SEL_KC:(cc + 1) * SEL_KC, LANES:2 * LANES].astype(F32)
            vst_ref[cc] = blk.T[hd:, :].astype(BF16)
        for cc in range(vwt_ref.shape[0]):
            blk = kv_ref[cc * qn:(cc + 1) * qn, 2 * LANES:3 * LANES].astype(F32)
            vwt_ref[cc] = blk.T[hd:, :].astype(BF16)

    q2 = q_ref[...]
    qst = jnp.concatenate([_dot_nt(pselt_ref[r], q2) for r in range(rep)], axis=1).astype(BF16)
    tq = qb * qn + lax.broadcasted_iota(jnp.int32, (1, qn), 1)
    tq4 = jnp.concatenate([tq] * rep, axis=1)

    c0 = jnp.maximum(qb - WINDOW // qn, 0)
    wstart = pl.multiple_of(c0 * qn, qn)
    sw = _dot(kv_ref[pl.ds(wstart, WIN_KEYS), 2 * LANES:3 * LANES], qst)
    sw = sw + wb_ref[jnp.minimum(qb, WINDOW // qn)]

    ncp = kcvc_ref.shape[0]
    s = _dot(kcvc_ref[...], qst)
    n_i = lax.broadcasted_iota(jnp.int32, (ncp, 1), 0)
    p = _masked_softmax0(s, (n_i * CMP_STRIDE + (CMP_BLOCK - 1)) <= tq4)
    o_cmp = _dot(vct_ref[...], p.astype(BF16))
    psum = p[:, 0:qn]
    for r in range(1, rep):
        psum = psum + p[:, r * qn:(r + 1) * qn]
    imp = _dot_f32_rhs(ovt_ref[...], psum)

    ew = jnp.exp(sw - jnp.max(sw, axis=0, keepdims=True))
    lw = jnp.sum(ew, axis=0, keepdims=True)
    pw = ew.astype(BF16)

    j = lax.broadcasted_iota(jnp.int32, (n_sel, 1), 0)
    cur = tq // SEL_BLOCK
    forced = (j == 0) | (j == cur) | (j == cur - 1)
    score = jnp.where(forced, FORCE_VALUE, jnp.where(j * SEL_BLOCK <= tq, imp, MASK_VALUE))
    cnt = jnp.zeros((n_sel, qn), F32)
    for jp in range(n_sel):
        row = score[jp:jp + 1, :]
        earlier = jnp.where(j > jp, 1.0, 0.0)
        cnt = cnt + jnp.where(row > score, 1.0, jnp.where(row == score, earlier, 0.0))
    selbias = jnp.where(cnt < n_top, 0.0, MASK_VALUE)
    selbias = jnp.concatenate([selbias] * rep, axis=1)
    for cc in range(n_sel // blocks_per_chunk):
        sb_ref[cc] = selbias[cc * blocks_per_chunk:(cc + 1) * blocks_per_chunk, :]

    o_win = _dot(vwt_ref[c0], pw[0:qn, :])
    for cc in range(1, WIN_KEYS // qn):
        o_win = o_win + _dot(vwt_ref[c0 + cc], pw[cc * qn:(cc + 1) * qn, :])
    o_win = o_win / lw

    m_ref[...] = jnp.full(m_ref.shape, MASK_VALUE, F32)
    l_ref[...] = jnp.zeros(l_ref.shape, F32)
    acc_ref[...] = jnp.zeros(acc_ref.shape, F32)

    def sel_chunk(c, diagonal):
        start = pl.multiple_of(c * SEL_KC, SEL_KC)
        sc = _dot(kv_ref[pl.ds(start, SEL_KC), LANES:2 * LANES], qst)
        bias = sb_ref[c]
        sc = jnp.concatenate([sc[b * SEL_BLOCK:(b + 1) * SEL_BLOCK, :] + bias[b:b + 1, :]
                              for b in range(blocks_per_chunk)], axis=0)
        if diagonal:
            sc = sc + db_ref[qb % (SEL_KC // qn)]
        m_old = m_ref[...]
        m_new = jnp.maximum(m_old, jnp.max(sc, axis=0, keepdims=True))
        alpha = jnp.exp(m_old - m_new)
        pc = jnp.exp(sc - m_new)
        l_ref[...] = alpha * l_ref[...] + jnp.sum(pc, axis=0, keepdims=True)
        acc_ref[...] = alpha * acc_ref[...] + _dot(vst_ref[c], pc.astype(BF16))
        m_ref[...] = m_new

    last = (qb * qn) // SEL_KC

    def full_chunk(c, carry):
        sel_chunk(c, False)
        return carry

    lax.fori_loop(0, last, full_chunk, 0)
    sel_chunk(last, True)
    o_sel = acc_ref[...] / l_ref[...]

    gs = _sigmoid(gate_ref[...])
    out = None
    for r in range(rep):
        sl = slice(r * qn, (r + 1) * qn)
        o_r = (gs[3 * r:3 * r + 1, :] * o_cmp[:, sl] + gs[3 * r + 1:3 * r + 2, :] * o_sel[:, sl]
               + gs[3 * r + 2:3 * r + 3, :] * o_win[:, sl])
        term = _dot(o_r.T.astype(BF16), qout_ref[r])
        out = term if out is None else out + term
    o_ref[...] = out.astype(o_ref.dtype)


def _nsa(q, kv, kcvc, vct, gates_t, seq):
    t = q.shape[0]
    bsz = t // seq
    nqb = seq // Q_BLOCK
    n_sel = seq // SEL_BLOCK
    n_top = min(SEL_TOP, n_sel)
    ncp = seq // CMP_STRIDE
    assert seq >= WIN_KEYS and seq % SEL_KC == 0 and SEL_KC % Q_BLOCK == 0 and n_sel % 8 == 0
    pselt, qout, overlap_t, win_bias, diag_bias = _nsa_consts(seq)
    gw = NSA_REP * HEAD_DIM
    nq = NSA_REP * Q_BLOCK
    nck = seq // SEL_KC
    body = functools.partial(_nsa_body, n_top=n_top, n_sel=n_sel)
    return pl.pallas_call(
        body,
        grid=(bsz, NSA_GROUPS, nqb),
        in_specs=[pl.BlockSpec((Q_BLOCK, gw), lambda b, g, i: (b * nqb + i, g)),
                  pl.BlockSpec((seq, 3 * LANES), lambda b, g, i: (b, g)),
                  pl.BlockSpec((None, None, ncp, LANES), lambda b, g, i: (b, g, 0, 0)),
                  pl.BlockSpec((None, None, HEAD_DIM, ncp), lambda b, g, i: (b, g, 0, 0)),
                  pl.BlockSpec((None, None, 3 * NSA_REP, Q_BLOCK), lambda b, g, i: (b, g, 0, i)),
                  pl.BlockSpec(pselt.shape, lambda b, g, i: (0, 0, 0)),
                  pl.BlockSpec(qout.shape, lambda b, g, i: (0, 0, 0)),
                  pl.BlockSpec(overlap_t.shape, lambda b, g, i: (0, 0)),
                  pl.BlockSpec(win_bias.shape, lambda b, g, i: (0, 0, 0)),
                  pl.BlockSpec(diag_bias.shape, lambda b, g, i: (0, 0, 0))],
        out_specs=pl.BlockSpec((Q_BLOCK, gw), lambda b, g, i: (b * nqb + i, g)),
        out_shape=jax.ShapeDtypeStruct((t, NSA_HEADS * HEAD_DIM), BF16),
        scratch_shapes=[pltpu.VMEM((nck, HEAD_DIM, SEL_KC), BF16),
                        pltpu.VMEM((nqb, HEAD_DIM, Q_BLOCK), BF16),
                        pltpu.VMEM((nck, SEL_KC // SEL_BLOCK, nq), F32),
                        pltpu.VMEM((1, nq), F32),
                        pltpu.VMEM((1, nq), F32),
                        pltpu.VMEM((HEAD_DIM, nq), F32)],
        compiler_params=_params("arbitrary", "arbitrary", "arbitrary"),
        name="nsa",
    )(q, kv, kcvc, vct, gates_t, pselt, qout, overlap_t, win_bias, diag_bias)


def _dn_body(qkv_ref, sm_ref, z_ref, cw_ref, hp_ref, ng_ref, ones_ref, tri_ref, o_ref,
             state_ref, prev_ref):
    c = pl.program_id(1)
    ch, hd = DN_CHUNK, HEAD_DIM
    width = DN_HEADS * hd
    nb = qkv_ref.shape[0]
    npair = DN_HEADS // 2
    units = [(b, p) for b in range(nb) for p in range(npair)]

    @pl.when(c == 0)
    def _():
        state_ref[...] = jnp.zeros(state_ref.shape, F32)
        prev_ref[...] = jnp.zeros(prev_ref.shape, F32)

    lane = lax.broadcasted_iota(jnp.int32, (1, LANES), 1)
    first = lane < hd
    ri = lax.broadcasted_iota(jnp.int32, (2 * ch, 1), 0)
    ci = lax.broadcasted_iota(jnp.int32, (1, 2 * ch), 1)
    same = (ri // ch) == (ci // ch)
    causal = same & ((ri % ch) >= (ci % ch))
    strict = same & ((ri % ch) > (ci % ch))
    blockdiag = (lax.broadcasted_iota(jnp.int32, (LANES, 1), 0) // hd) == (lane // hd)
    ones_blk = ones_ref[...]

    def stack(v):
        return jnp.concatenate([jnp.where(first, v, 0.0), jnp.where(first, 0.0, v)], axis=0)

    def fold(v):
        return v[:ch] + v[ch:]

    def head_sumsq(v):
        return _dot((v * v).astype(BF16), ones_blk)

    ys, betas, gcs = [], [], []
    for b in range(nb):
        x = qkv_ref[b].astype(F32)
        xe = jnp.concatenate([prev_ref[b], x], axis=0)
        y = cw_ref[DN_CONV - 1:DN_CONV, :] * x
        for tap in range(DN_CONV - 1):
            y = y + cw_ref[tap:tap + 1, :] * pltpu.roll(xe, DN_CONV - 1 - tap, 0)[8:, :]
        prev_ref[b] = x[ch - 8:, :]
        ys.append(y * _sigmoid(y))
        sm = sm_ref[b]
        betas.append(_sigmoid(sm))
        xa = sm + hp_ref[1:2, :]
        softplus = jnp.maximum(xa, 0.0) + jnp.log(1.0 + jnp.exp(-jnp.abs(xa)))
        gcs.append(_dot_f32_rhs(tri_ref[...], -jnp.exp(hp_ref[0:1, :]) * softplus))

    def bc(tile, base, p):
        return jnp.where(first, tile[:, base + 2 * p:base + 2 * p + 1], tile[:, base + 2 * p + 1:base + 2 * p + 2])

    qn, kn, vb, kb, gc, egc, dec = {}, {}, {}, {}, {}, {}, {}
    for u in units:
        b, p = u
        y = ys[b]
        qp = y[:, p * LANES:(p + 1) * LANES]
        kp = y[:, width + p * LANES:width + (p + 1) * LANES]
        vp = y[:, 2 * width + p * LANES:2 * width + (p + 1) * LANES]
        qn[u] = qp * lax.rsqrt(head_sumsq(qp) + EPS) * (hd ** -0.5)
        kn[u] = kp * lax.rsqrt(head_sumsq(kp) + EPS)
        beta = bc(betas[b], SM_BETA, p)
        gc[u] = bc(gcs[b], SM_A, p)
        egc[u] = jnp.exp(gc[u])
        kb[u] = kn[u] * beta
        vb[u] = vp * beta
        gcol = jnp.concatenate([gcs[b][:, SM_A + 2 * p:SM_A + 2 * p + 1],
                                gcs[b][:, SM_A + 2 * p + 1:SM_A + 2 * p + 2]], axis=0)
        gmat = jnp.broadcast_to(gcol, (2 * ch, 2 * ch))
        dec[u] = jnp.where(causal, jnp.exp(jnp.where(causal, gmat - gmat.T, 0.0)), 0.0)

    nmat, attn, sol = {}, {}, {}
    for u in units:
        ks = jnp.concatenate([kn[u], kn[u]], axis=0).astype(BF16)
        nmat[u] = -(_dot_nt(stack(kb[u]).astype(BF16), ks) * jnp.where(strict, dec[u], 0.0))
        attn[u] = (_dot_nt(stack(qn[u]).astype(BF16), ks) * dec[u]).astype(BF16)
        sol[u] = jnp.concatenate([stack(vb[u]), stack(kb[u] * egc[u])], axis=1)

    base = 8
    ii, jj = ri % ch, ci % ch
    eye = jnp.where((ri == ci), 1.0, 0.0)
    tinv = {}
    for u in units:
        n1 = (nmat[u] * jnp.where(same & ((ii // base) == (jj // base)), 1.0, 0.0)).astype(BF16)
        tinv[u] = eye + n1.astype(F32)
        nmat[u] = -nmat[u]
        n2 = _dot(n1, n1).astype(BF16)
        tinv[u] = tinv[u] + _dot(n2, tinv[u].astype(BF16))
        n4 = _dot(n2, n2).astype(BF16)
        tinv[u] = tinv[u] + _dot(n4, tinv[u].astype(BF16))
    s = base
    while s < ch:
        lmask = jnp.where(same & ((ii // (2 * s)) == (jj // (2 * s)))
                          & (((ii // s) % 2) == 1) & (((jj // s) % 2) == 0), 1.0, 0.0)
        for u in units:
            tb = tinv[u].astype(BF16)
            tl = _dot(tb, (nmat[u] * lmask).astype(BF16)).astype(BF16)
            tinv[u] = tinv[u] - _dot(tl, tb)
        s *= 2
    for u in units:
        sol[u] = _dot(tinv[u].astype(BF16), sol[u].astype(BF16))

    v_new, st = {}, {}
    for u in units:
        b, p = u
        st[u] = state_ref[b, p]
        v_new[u] = fold(sol[u][:, :LANES]) - _dot(fold(sol[u][:, LANES:]).astype(BF16), st[u].astype(BF16))

    o = {}
    for u in units:
        b, p = u
        glast = gc[u][ch - 1:ch, :]
        o[u] = (_dot((qn[u] * egc[u]).astype(BF16), st[u].astype(BF16))
                + fold(_dot(attn[u], stack(v_new[u]).astype(BF16))))
        k_dec = kn[u] * jnp.exp(glast - gc[u])
        upd = _dot(k_dec.T.astype(BF16), v_new[u].astype(BF16))
        state_ref[b, p] = st[u] * jnp.exp(glast) + jnp.where(blockdiag, upd, 0.0)

    for u in units:
        b, p = u
        sl = slice(p * LANES, (p + 1) * LANES)
        on = o[u] * lax.rsqrt(head_sumsq(o[u]) * (1.0 / hd) + EPS) * ng_ref[...]
        zp = z_ref[b, :, sl].astype(F32)
        o_ref[b, :, sl] = (on * (zp * _sigmoid(zp))).astype(o_ref.dtype)


def _dn(qkv, sm, z, conv_w, hp, ng, seq):
    t = qkv.shape[0]
    bsz = t // seq
    nb = 2 if bsz % 2 == 0 else 1
    nc = seq // DN_CHUNK
    width = DN_HEADS * HEAD_DIM
    ones_blk = jnp.asarray(np.kron(np.eye(2), np.ones((HEAD_DIM, HEAD_DIM))), BF16)
    tri = jnp.asarray(np.tril(np.ones((DN_CHUNK, DN_CHUNK))), BF16)
    y = pl.pallas_call(
        _dn_body,
        grid=(bsz // nb, nc),
        in_specs=[pl.BlockSpec((nb, DN_CHUNK, 3 * width), lambda b, c: (b, c, 0)),
                  pl.BlockSpec((nb, DN_CHUNK, LANES), lambda b, c: (b, c, 0)),
                  pl.BlockSpec((nb, DN_CHUNK, width), lambda b, c: (b, c, 0)),
                  pl.BlockSpec(conv_w.shape, lambda b, c: (0, 0)),
                  pl.BlockSpec(hp.shape, lambda b, c: (0, 0)),
                  pl.BlockSpec(ng.shape, lambda b, c: (0, 0)),
                  pl.BlockSpec(ones_blk.shape, lambda b, c: (0, 0)),
                  pl.BlockSpec(tri.shape, lambda b, c: (0, 0))],
        out_specs=pl.BlockSpec((nb, DN_CHUNK, width), lambda b, c: (b, c, 0)),
        out_shape=jax.ShapeDtypeStruct((bsz, seq, width), BF16),
        scratch_shapes=[pltpu.VMEM((nb, DN_HEADS // 2, LANES, LANES), F32),
                        pltpu.VMEM((nb, 8, 3 * width), F32)],
        compiler_params=_params("arbitrary", "arbitrary"),
        name="dn",
    )(qkv.reshape(bsz, seq, 3 * width), sm.reshape(bsz, seq, LANES), z.reshape(bsz, seq, width),
      conv_w, hp, ng, ones_blk, tri)
    return y.reshape(t, width)


def _merge_body(yn_ref, yd_ref, mg_ref, x_ref, mod_ref, gf_ref, wb_ref, wo_ref, rwh_ref, rwl_ref,
                rb_ref, tri_ref, x1_ref, h_ref, rt_ref, cnt_ref):
    i = pl.program_id(0)
    d = x_ref.shape[1]

    @pl.when(i == 0)
    def _():
        cnt_ref[...] = jnp.zeros(cnt_ref.shape, F32)

    br0 = _dot(yn_ref[...], wb_ref[0])
    br1 = _dot(yd_ref[...], wb_ref[1])
    mixin = (_sigmoid(mg_ref[:, :d].astype(F32)) * br0 + _sigmoid(mg_ref[:, d:].astype(F32)) * br1)
    mix = _dot(mixin.astype(BF16), wo_ref[...])
    x1 = x_ref[...] + mod_ref[2:3, :] * mix
    x1_ref[...] = x1
    h = _rms_mod(x1, gf_ref[...], mod_ref[3:4, :], mod_ref[4:5, :])
    hh = h.astype(BF16)
    hw = pltpu.bitcast(hh.astype(F32), jnp.uint32)
    h_ref[...] = (hw[:, :d // 2] >> 16) | (hw[:, d // 2:] & jnp.uint32(0xFFFF0000))

    hl = (h - hh.astype(F32)).astype(BF16)
    logits = _dot(hh, rwh_ref[...]) + _dot(hh, rwl_ref[...]) + _dot(hl, rwh_ref[...]) + rb_ref[...]
    lane = lax.broadcasted_iota(jnp.int32, (1, LANES), 1)
    cur = jnp.where(lane < N_EXPERTS, logits, PAD_SCORE)
    vals, idxs = [], []
    for _ in range(TOP_K):
        m = jnp.max(cur, axis=-1, keepdims=True)
        ix = jnp.min(jnp.where(cur == m, lane, LANES), axis=-1, keepdims=True)
        vals.append(m)
        idxs.append(ix)
        cur = jnp.where(lane == ix, PAD_SCORE, cur)
    es = [jnp.exp(v - vals[0]) for v in vals]
    den = es[0] + es[1] + es[2] + es[3]

    onehot = jnp.zeros(logits.shape, F32)
    for ix in idxs:
        onehot = onehot + jnp.where(lane == ix, 1.0, 0.0)
    before = _dot(tri_ref[...], onehot.astype(BF16)) + cnt_ref[...]
    cnt_ref[...] = cnt_ref[...] + jnp.sum(onehot, axis=0, keepdims=True)

    rt = jnp.zeros(logits.shape, F32)
    for k in range(TOP_K):
        rank = jnp.sum(jnp.where(lane == idxs[k], before, 0.0), axis=-1, keepdims=True)
        rt = jnp.where(lane == RT_IDX + k, idxs[k].astype(F32), rt)
        rt = jnp.where(lane == RT_W + k, es[k] / den, rt)
        rt = jnp.where(lane == RT_RANK + k, rank, rt)
    rt_ref[...] = rt


def _merge(y_nsa, y_dn, mg, x2, mod3, g_ffn, wb, wo, rwh, rwl, rb, seq):
    t, d = x2.shape
    tm = 512
    per_b = seq // tm
    hw = y_nsa.shape[1]
    tri = jnp.asarray(np.tril(np.ones((tm, tm)), -1), BF16)
    return pl.pallas_call(
        _merge_body,
        grid=(t // tm,),
        in_specs=[pl.BlockSpec((tm, hw), lambda i: (i, 0)),
                  pl.BlockSpec((tm, hw), lambda i: (i, 0)),
                  pl.BlockSpec((tm, 2 * d), lambda i: (i, 0)),
                  pl.BlockSpec((tm, d), lambda i: (i, 0)),
                  pl.BlockSpec((None, 6, d), lambda i: (i // per_b, 0, 0)),
                  pl.BlockSpec((1, d), lambda i: (0, 0)),
                  pl.BlockSpec(wb.shape, lambda i: (0, 0, 0)),
                  pl.BlockSpec(wo.shape, lambda i: (0, 0)),
                  pl.BlockSpec(rwh.shape, lambda i: (0, 0)),
                  pl.BlockSpec(rwl.shape, lambda i: (0, 0)),
                  pl.BlockSpec(rb.shape, lambda i: (0, 0)),
                  pl.BlockSpec(tri.shape, lambda i: (0, 0))],
        out_specs=[pl.BlockSpec((tm, d), lambda i: (i, 0)),
                   pl.BlockSpec((tm, d // 2), lambda i: (i, 0)),
                   pl.BlockSpec((tm, LANES), lambda i: (i, 0)),
                   pl.BlockSpec((1, LANES), lambda i: (0, 0))],
        out_shape=[jax.ShapeDtypeStruct((t, d), F32),
                   jax.ShapeDtypeStruct((t, d // 2), jnp.uint32),
                   jax.ShapeDtypeStruct((t, LANES), F32),
                   jax.ShapeDtypeStruct((1, LANES), F32)],
        compiler_params=_params("arbitrary"),
        name="merge",
    )(y_nsa, y_dn, mg, x2, mod3, g_ffn.reshape(1, d), wb, wo, rwh, rwl, rb, tri)


def _row_copy(src, src_row, dst, dst_row, sem):
    return pltpu.make_async_copy(src.at[pl.ds(src_row, 1), :], dst.at[pl.ds(dst_row, 1), :], sem)


def _scatter_body(pad_lo_ref, pad_hi_ref, slot_ref, h_ref, xs_ref, zero_ref, sem, zsem):
    tm = h_ref.shape[0]

    @pl.when(pl.program_id(0) == 0)
    def _():
        zero_ref[...] = jnp.zeros(zero_ref.shape, zero_ref.dtype)
        for e in range(N_EXPERTS):
            def zissue(r, carry):
                _row_copy(zero_ref, 0, xs_ref, r, zsem).start()
                return carry

            def zdrain(r, carry):
                _row_copy(zero_ref, 0, xs_ref, r, zsem).wait()
                return carry

            lax.fori_loop(pad_lo_ref[e], pad_hi_ref[e], zissue, 0)
            lax.fori_loop(pad_lo_ref[e], pad_hi_ref[e], zdrain, 0)

    def issue(r, carry):
        for k in range(TOP_K):
            _row_copy(h_ref, r, xs_ref, slot_ref[0, r * TOP_K + k], sem).start(priority=k % 2)
        return carry

    lax.fori_loop(0, tm, issue, 0)

    def drain(r, carry):
        for k in range(TOP_K):
            _row_copy(h_ref, r, xs_ref, slot_ref[0, r * TOP_K + k], sem).wait()
        return carry

    lax.fori_loop(0, tm, drain, 0)


def _scatter(pad_lo, pad_hi, slots3, h, n_rows):
    t, d = h.shape
    tm = SCATTER_TILE
    grid_spec = pltpu.PrefetchScalarGridSpec(
        num_scalar_prefetch=2,
        grid=(t // tm,),
        in_specs=[pl.BlockSpec((None, 1, tm * TOP_K), lambda i, lo, hi: (i, 0, 0), memory_space=pltpu.SMEM),
                  pl.BlockSpec((tm, d), lambda i, lo, hi: (i, 0))],
        out_specs=pl.BlockSpec(memory_space=pl.ANY),
        scratch_shapes=[pltpu.VMEM((8, d), h.dtype), pltpu.SemaphoreType.DMA(()), pltpu.SemaphoreType.DMA(())],
    )
    return pl.pallas_call(
        _scatter_body,
        grid_spec=grid_spec,
        out_shape=jax.ShapeDtypeStruct((n_rows, d), h.dtype),
        compiler_params=_params("arbitrary"),
        name="scatter",
    )(pad_lo, pad_hi, slots3, h)


def _experts_body(te_ref, nv_ref, xs_ref, w1_ref, b1_ref, w2_ref, b2_ref, ys_ref):
    i = pl.program_id(0)
    f = w2_ref.shape[0]

    @pl.when(i < nv_ref[0])
    def _():
        xw = xs_ref[...]
        half = xw.shape[1]
        x_lo = pltpu.bitcast(xw << 16, F32).astype(BF16)
        x_hi = pltpu.bitcast(xw & jnp.uint32(0xFFFF0000), F32).astype(BF16)
        u = _dot(x_lo, w1_ref[:half, :]) + _dot(x_hi, w1_ref[half:, :]) + b1_ref[...]
        x_glu = jnp.minimum(u[:, :f], SWIGLU_LIMIT)
        x_lin = jnp.clip(u[:, f:], -SWIGLU_LIMIT, SWIGLU_LIMIT)
        act = x_glu * _sigmoid(SWIGLU_ALPHA * x_glu) * (x_lin + 1.0)
        y = _dot(act.astype(BF16), w2_ref[...]) + b2_ref[...]
        yw = pltpu.bitcast(y.astype(BF16).astype(F32), jnp.uint32)
        ys_ref[...] = (yw[:, :half] >> 16) | (yw[:, half:] & jnp.uint32(0xFFFF0000))

    @pl.when(i >= nv_ref[0])
    def _():
        ys_ref[...] = jnp.zeros(ys_ref.shape, ys_ref.dtype)


def _experts(tile_expert, n_valid, xs, w1, b1, w2, b2):
    p = xs.shape[0]
    d = w1.shape[1]
    tm = EXPERT_TILE
    f = w2.shape[1]
    grid_spec = pltpu.PrefetchScalarGridSpec(
        num_scalar_prefetch=2,
        grid=(p // tm,),
        in_specs=[pl.BlockSpec((tm, xs.shape[1]), lambda i, te, nv: (jnp.minimum(i, nv[0] - 1), 0)),
                  pl.BlockSpec((None, d, 2 * f), lambda i, te, nv: (te[i], 0, 0)),
                  pl.BlockSpec((None, 1, 2 * f), lambda i, te, nv: (te[i], 0, 0)),
                  pl.BlockSpec((None, f, d), lambda i, te, nv: (te[i], 0, 0)),
                  pl.BlockSpec((None, 1, d), lambda i, te, nv: (te[i], 0, 0))],
        out_specs=pl.BlockSpec((tm, d // 2), lambda i, te, nv: (i, 0)),
    )
    return pl.pallas_call(
        _experts_body,
        grid_spec=grid_spec,
        out_shape=jax.ShapeDtypeStruct((p, d // 2), jnp.uint32),
        compiler_params=_params("arbitrary"),
        name="experts",
    )(tile_expert, n_valid, xs, w1, b1, w2, b2)


def _combine_body(slot_ref, rt_ref, x1_ref, mod_ref, g_ref, ys_ref, o_ref, buf_ref, sem):
    tm = x1_ref.shape[0]

    def issue(r, carry):
        for k in range(TOP_K):
            _row_copy(ys_ref, slot_ref[0, r * TOP_K + k], buf_ref.at[k], r, sem).start(priority=k % 2)
        return carry

    lax.fori_loop(0, tm, issue, 0)

    def drain(r, carry):
        for k in range(TOP_K):
            _row_copy(ys_ref, slot_ref[0, r * TOP_K + k], buf_ref.at[k], r, sem).wait()
        return carry

    lax.fori_loop(0, tm, drain, 0)

    rt = rt_ref[...]
    moe_lo, moe_hi = None, None
    for k in range(TOP_K):
        yw = buf_ref[k]
        wk = rt[:, RT_W + k:RT_W + k + 1]
        lo = wk * pltpu.bitcast(yw << 16, F32)
        hi = wk * pltpu.bitcast(yw & jnp.uint32(0xFFFF0000), F32)
        moe_lo = lo if moe_lo is None else moe_lo + lo
        moe_hi = hi if moe_hi is None else moe_hi + hi
    moe = jnp.concatenate([moe_lo, moe_hi], axis=1)
    x2 = x1_ref[...] + mod_ref[5:6, :] * moe
    ms = jnp.mean(x2 * x2, axis=-1, keepdims=True)
    o_ref[...] = x2 * lax.rsqrt(ms + EPS) * g_ref[...]


def _combine(slots3, rt, x1, mod3, g_final, ys, seq):
    t, d = x1.shape
    tm = SCATTER_TILE
    per_b = seq // tm
    return pl.pallas_call(
        _combine_body,
        grid=(t // tm,),
        in_specs=[pl.BlockSpec((None, 1, tm * TOP_K), lambda i: (i, 0, 0), memory_space=pltpu.SMEM),
                  pl.BlockSpec((tm, LANES), lambda i: (i, 0)),
                  pl.BlockSpec((tm, d), lambda i: (i, 0)),
                  pl.BlockSpec((None, 6, d), lambda i: (i // per_b, 0, 0)),
                  pl.BlockSpec((1, d), lambda i: (0, 0)),
                  pl.BlockSpec(memory_space=pl.ANY)],
        out_specs=pl.BlockSpec((tm, d), lambda i: (i, 0)),
        out_shape=jax.ShapeDtypeStruct((t, d), F32),
        scratch_shapes=[pltpu.VMEM((TOP_K, tm, ys.shape[1]), ys.dtype), pltpu.SemaphoreType.DMA(())],
        compiler_params=_params("arbitrary"),
        name="combine",
    )(slots3, rt, x1, mod3, g_final.reshape(1, d), ys)


def _pad_lanes(v, offset):
    out = jnp.zeros((1, LANES), F32)
    return out.at[0, offset:offset + v.shape[0]].set(v.astype(F32))


def kernel(x, c, w_ada, b_ada, g_norm_mix, w_in, cmp_pe_k, cmp_pe_v, cmp_w1, cmp_b1, cmp_w2,
           dn_conv_w, dn_a_log, dn_dt_bias, dn_norm_g, w_branch, w_out, g_norm_ffn,
           router_w, router_b, exp_w1, exp_b1, exp_w2, exp_b2, final_norm_g):
    bsz, seq, d = x.shape
    t = bsz * seq
    depth = w_ada.shape[0]
    assert depth == 1, "the final norm is fused into the last layer's combine step"
    x2 = x.reshape(t, d)
    cols = _in_columns()
    out = None
    for l in range(depth):
        mod3 = _ada(c, w_ada[l], b_ada[l]).reshape(bsz, 6, d)

        w_big = jnp.where(jnp.asarray(cols >= 0)[None, :], w_in[l][:, np.maximum(cols, 0)], 0.0).astype(BF16)
        q, kv, dnqkv, z, mg, sm = _inproj(x2, mod3, g_norm_mix[l], w_big, seq)

        nrow = seq // CMP_STRIDE
        src = kv.reshape(bsz, nrow, CMP_STRIDE, NSA_GROUPS, 6, HEAD_DIM)[:, :, :, :, 0:2, :]
        src = src.transpose(0, 3, 4, 1, 2, 5).reshape(bsz, NSA_GROUPS, 2, nrow, CMP_STRIDE * HEAD_DIM)
        pe = jnp.stack([cmp_pe_k[l], cmp_pe_v[l]]).reshape(2, 1, CMP_BLOCK * HEAD_DIM)
        pe = jnp.broadcast_to(pe, (2, 8, CMP_BLOCK * HEAD_DIM)).astype(BF16)
        w2p = jnp.zeros((2, CMP_HIDDEN, LANES), F32)
        w2p = w2p.at[0, :, :HEAD_DIM].set(cmp_w2[l, 0]).at[1, :, HEAD_DIM:].set(cmp_w2[l, 1]).astype(BF16)
        kcvc, vct = _cmp(src, cmp_w1[l].astype(BF16), pe, cmp_b1[l].reshape(2, 1, CMP_HIDDEN), w2p,
                         cmp_w2[l, 1].T.astype(BF16))

        gates_t = sm[:, SM_GATE:SM_GATE + 3 * NSA_HEADS].reshape(bsz, seq, NSA_GROUPS, 3 * NSA_REP)
        gates_t = gates_t.transpose(0, 2, 3, 1)
        y_nsa = _nsa(q, kv, kcvc, vct, gates_t, seq)

        hp = jnp.concatenate([_pad_lanes(dn_a_log[l], SM_A), _pad_lanes(dn_dt_bias[l], SM_A),
                              jnp.zeros((6, LANES), F32)], axis=0)
        ng = jnp.tile(dn_norm_g[l].reshape(1, HEAD_DIM), (1, 2))
        y_dn = _dn(dnqkv, sm, z, dn_conv_w[l], hp, ng, seq)

        rw = jnp.zeros((d, LANES), F32).at[:, :N_EXPERTS].set(router_w[l])
        rwh = rw.astype(BF16)
        rwl = (rw - rwh.astype(F32)).astype(BF16)
        x1, h, rt, cnt = _merge(y_nsa, y_dn, mg, x2, mod3, g_norm_ffn[l], w_branch[l].astype(BF16),
                                w_out[l].astype(BF16), rwh, rwl, _pad_lanes(router_b[l], 0), seq)

        counts = cnt[0, :N_EXPERTS].astype(jnp.int32)
        tiles_per = (counts + EXPERT_TILE - 1) // EXPERT_TILE
        tile_end = jnp.cumsum(tiles_per)
        offs = (tile_end - tiles_per) * EXPERT_TILE
        n_rows = t * TOP_K + N_EXPERTS * EXPERT_TILE
        n_tiles = n_rows // EXPERT_TILE
        idx = rt[:, RT_IDX:RT_IDX + TOP_K].astype(jnp.int32)
        rank = rt[:, RT_RANK:RT_RANK + TOP_K].astype(jnp.int32)
        slots = offs[idx] + rank
        slots3 = slots.reshape(t // SCATTER_TILE, 1, SCATTER_TILE * TOP_K)
        tile_ids = jnp.arange(n_tiles, dtype=jnp.int32)
        tile_expert = jnp.minimum(jnp.sum((tile_ids[:, None] >= tile_end[None, :]).astype(jnp.int32), axis=1),
                                  N_EXPERTS - 1).astype(jnp.int32)
        n_valid = tile_end[-1:].astype(jnp.int32)

        xs = _scatter(offs + counts, offs + tiles_per * EXPERT_TILE, slots3, h, n_rows)
        ys = _experts(tile_expert, n_valid, xs, exp_w1[l].astype(BF16),
                      exp_b1[l].reshape(N_EXPERTS, 1, -1), exp_w2[l].astype(BF16),
                      exp_b2[l].reshape(N_EXPERTS, 1, -1))
        out = _combine(slots3, rt, x1, mod3, final_norm_g, ys, seq)
    return out.reshape(bsz, seq, d)
```

```python
import functools

import numpy as np
import jax
import jax.numpy as jnp
from jax import lax
from jax.experimental import pallas as pl
from jax.experimental.pallas import tpu as pltpu

F32 = jnp.float32
BF16 = jnp.bfloat16

HEAD_DIM = 64
NSA_HEADS = 8
NSA_GROUPS = 2
NSA_REP = NSA_HEADS // NSA_GROUPS
CMP_BLOCK = 32
CMP_STRIDE = 16
CMP_HIDDEN = 256
SEL_BLOCK = 64
SEL_TOP = 16
WINDOW = 512
Q_BLOCK = 256
DN_HEADS = 8
DN_CONV = 4
DN_CHUNK = 64
N_EXPERTS = 32
TOP_K = 4
SWIGLU_LIMIT = 7.0
SWIGLU_ALPHA = 1.702
EPS = 1e-6
MASK_VALUE = -1e30
FORCE_VALUE = 1e9
PAD_SCORE = -3e38

LANES = 128
VMEM_LIMIT = 56 * 2 ** 20

SEL_KC = 512
WIN_KEYS = WINDOW + Q_BLOCK
EXPERT_TILE = 512
SCATTER_TILE = 256

SM_GATE, SM_BETA, SM_A = 0, 24, 32
RT_IDX, RT_W, RT_RANK = 0, 4, 8


def _dot(a, b):
    return jnp.dot(a, b, preferred_element_type=F32)


def _dot_nt(a, b):
    return lax.dot_general(a, b, (((1,), (1,)), ((), ())), preferred_element_type=F32)


def _split3(x):
    hi = x.astype(BF16)
    r1 = x - hi.astype(F32)
    mid = r1.astype(BF16)
    lo = (r1 - mid.astype(F32)).astype(BF16)
    return hi, mid, lo


def _dot_f32_lhs(x, w_bf16):
    hi, mid, lo = _split3(x)
    return _dot(hi, w_bf16) + _dot(mid, w_bf16) + _dot(lo, w_bf16)


def _dot_f32_rhs(w_bf16, x):
    hi, mid, lo = _split3(x)
    return _dot(w_bf16, hi) + _dot(w_bf16, mid) + _dot(w_bf16, lo)


def _sigmoid(x):
    return 1.0 / (1.0 + jnp.exp(-x))


def _params(*sem):
    return pltpu.CompilerParams(dimension_semantics=sem, vmem_limit_bytes=VMEM_LIMIT)


def _ada_body(c_ref, w_ref, b_ref, o_ref):
    c = c_ref[...]
    a = (c * _sigmoid(c)).astype(BF16)
    o_ref[...] = _dot(a, w_ref[...].astype(BF16)) + b_ref[...]


def _ada(c, w, b):
    bsz, d = c.shape
    n = w.shape[1]
    tn = 1024
    return pl.pallas_call(
        _ada_body,
        grid=(n // tn,),
        in_specs=[pl.BlockSpec((bsz, d), lambda j: (0, 0)),
                  pl.BlockSpec((d, tn), lambda j: (0, j)),
                  pl.BlockSpec((1, tn), lambda j: (0, j))],
        out_specs=pl.BlockSpec((bsz, tn), lambda j: (0, j)),
        out_shape=jax.ShapeDtypeStruct((bsz, n), F32),
        compiler_params=_params("arbitrary"),
        name="ada",
    )(c, w, b.reshape(1, n))


IN_SEGS = (("q", 512, BF16), ("kv", 768, BF16), ("dn", 1536, BF16),
           ("z", 512, BF16), ("mg", 2048, BF16), ("sm", LANES, F32))


def _in_columns():
    q0 = 0
    kv0 = q0 + 512
    gate0 = kv0 + 768
    dn0 = gate0 + 24
    beta0 = dn0 + 1536
    a0 = beta0 + 8
    z0 = a0 + 8
    mg0 = z0 + 512
    cols = list(range(q0, q0 + 512))
    for g in range(NSA_GROUPS):
        for i in range(6):
            base = kv0 + i * NSA_GROUPS * HEAD_DIM + g * HEAD_DIM
            cols += list(range(base, base + HEAD_DIM))
    cols += list(range(dn0, dn0 + 1536))
    cols += list(range(z0, z0 + 512))
    cols += list(range(mg0, mg0 + 2048))
    small = list(range(gate0, gate0 + 24)) + list(range(beta0, beta0 + 8)) + list(range(a0, a0 + 8))
    cols += small + [-1] * (LANES - len(small))
    return np.asarray(cols, np.int32)


def _rms_mod(x, g, shift, scale):
    ms = jnp.mean(x * x, axis=-1, keepdims=True)
    y = x * lax.rsqrt(ms + EPS) * g
    return y * (1.0 + scale) + shift


def _inproj_body(x_ref, mod_ref, g_ref, w_ref, *out_refs):
    h = _rms_mod(x_ref[...], g_ref[...], mod_ref[0:1, :], mod_ref[1:2, :])
    hb = h.astype(BF16)
    off = 0
    for ref, (_, width, _) in zip(out_refs, IN_SEGS):
        for c0 in range(0, width, 512):
            cw = min(512, width - c0)
            ref[:, c0:c0 + cw] = _dot(hb, w_ref[:, off + c0:off + c0 + cw]).astype(ref.dtype)
        off += width


def _inproj(x2, mod3, g, w_big, seq):
    t, d = x2.shape
    tm = 512
    per_b = seq // tm
    nw = w_big.shape[1]
    return pl.pallas_call(
        _inproj_body,
        grid=(t // tm,),
        in_specs=[pl.BlockSpec((tm, d), lambda i: (i, 0)),
                  pl.BlockSpec((None, 6, d), lambda i: (i // per_b, 0, 0)),
                  pl.BlockSpec((1, d), lambda i: (0, 0)),
                  pl.BlockSpec((d, nw), lambda i: (0, 0))],
        out_specs=[pl.BlockSpec((tm, w), lambda i: (i, 0)) for _, w, _ in IN_SEGS],
        out_shape=[jax.ShapeDtypeStruct((t, w), dt) for _, w, dt in IN_SEGS],
        compiler_params=_params("arbitrary"),
        name="inproj",
    )(x2, mod3, g.reshape(1, d), w_big)


def _cmp_body(src_ref, w1_ref, pe_ref, b1_ref, w2_ref, w2t_ref, o_ref, ot_ref):
    half = CMP_STRIDE * HEAD_DIM
    out = None
    for kind in range(2):
        x = src_ref[kind]
        first = _dot(x, w1_ref[kind, :half, :])
        second = _dot(x, w1_ref[kind, half:, :])
        n = second.shape[0]
        second = pltpu.roll(second, n - 1, 0)
        pew = _dot(pe_ref[kind], w1_ref[kind])[0:1, :]
        pre = first + second + pew + b1_ref[kind]
        hid = (pre * _sigmoid(pre)).astype(BF16)
        term = _dot(hid, w2_ref[kind])
        out = term if out is None else out + term
    o_ref[...] = out.astype(o_ref.dtype)
    ot_ref[...] = _dot_nt(w2t_ref[...], hid).astype(ot_ref.dtype)


def _cmp(src, w1, pe, b1, w2p, w2t):
    bsz, ng, _, nrow, width = src.shape
    return pl.pallas_call(
        _cmp_body,
        grid=(bsz, ng),
        in_specs=[pl.BlockSpec((None, None, 2, nrow, width), lambda b, g: (b, g, 0, 0, 0)),
                  pl.BlockSpec(w1.shape, lambda b, g: (0, 0, 0)),
                  pl.BlockSpec(pe.shape, lambda b, g: (0, 0, 0)),
                  pl.BlockSpec(b1.shape, lambda b, g: (0, 0, 0)),
                  pl.BlockSpec(w2p.shape, lambda b, g: (0, 0, 0)),
                  pl.BlockSpec(w2t.shape, lambda b, g: (0, 0))],
        out_specs=[pl.BlockSpec((None, None, nrow, LANES), lambda b, g: (b, g, 0, 0)),
                   pl.BlockSpec((None, None, HEAD_DIM, nrow), lambda b, g: (b, g, 0, 0))],
        out_shape=[jax.ShapeDtypeStruct((bsz, ng, nrow, LANES), BF16),
                   jax.ShapeDtypeStruct((bsz, ng, HEAD_DIM, nrow), BF16)],
        compiler_params=_params("arbitrary", "arbitrary"),
        name="cmp",
    )(src, w1, pe, b1, w2p, w2t)


def _nsa_consts(seq):
    scale = HEAD_DIM ** -0.5
    gw = NSA_REP * HEAD_DIM
    pselt = np.zeros((NSA_REP, LANES, gw), np.float32)
    qout = np.zeros((NSA_REP, HEAD_DIM, gw), np.float32)
    for r in range(NSA_REP):
        for d in range(HEAD_DIM):
            pselt[r, d, r * HEAD_DIM + d] = scale
            qout[r, d, r * HEAD_DIM + d] = 1.0
    n_cmp = (seq - CMP_BLOCK) // CMP_STRIDE + 1
    n_sel = seq // SEL_BLOCK
    cs = np.arange(n_cmp)[:, None] * CMP_STRIDE
    ss = np.arange(n_sel)[None, :] * SEL_BLOCK
    ov = np.clip(np.minimum(cs + CMP_BLOCK, ss + SEL_BLOCK) - np.maximum(cs, ss), 0, None) / CMP_BLOCK
    overlap_t = np.zeros((n_sel, seq // CMP_STRIDE), np.float32)
    overlap_t[:, :n_cmp] = ov.T
    qrel = np.tile(np.arange(Q_BLOCK), NSA_REP)[None, None, :]
    shift = (np.arange(WINDOW // Q_BLOCK + 1) * Q_BLOCK)[:, None, None]
    delta = shift + qrel - np.arange(WIN_KEYS)[None, :, None]
    win_bias = np.where((delta >= 0) & (delta < WINDOW), 0.0, MASK_VALUE).astype(np.float32)
    shift = (np.arange(SEL_KC // Q_BLOCK) * Q_BLOCK)[:, None, None]
    diag_bias = np.where(np.arange(SEL_KC)[None, :, None] <= shift + qrel, 0.0, MASK_VALUE).astype(np.float32)
    return (jnp.asarray(pselt, BF16), jnp.asarray(qout, BF16), jnp.asarray(overlap_t, BF16),
            jnp.asarray(win_bias), jnp.asarray(diag_bias))


def _masked_softmax0(s, allowed):
    s = jnp.where(allowed, s, MASK_VALUE)
    e = jnp.exp(s - jnp.max(s, axis=0, keepdims=True))
    p = e / jnp.sum(e, axis=0, keepdims=True)
    return jnp.where(allowed, p, 0.0)


def _nsa_body(q_ref, kv_ref, kcvc_ref, vct_ref, gate_ref, pselt_ref, qout_ref, ovt_ref, wb_ref, db_ref,
              o_ref, vst_ref, vwt_ref, sb_ref, m_ref, l_ref, acc_ref, *, n_top, n_sel):
    qb = pl.program_id(2)
    rep, qn, hd = NSA_REP, Q_BLOCK, HEAD_DIM
    nq = rep * qn
    blocks_per_chunk = SEL_KC // SEL_BLOCK

    @pl.when(qb == 0)
    def _():
        for cc in range(vst_ref.shape[0]):
            blk = kv_ref[cc * SEL_KC:(cc + 1) * SEL_KC, LANES:2 * LANES].astype(F32)
            vst_ref[cc] = blk.T[hd:, :].astype(BF16)
        for cc in range(vwt_ref.shape[0]):
            blk = kv_ref[cc * qn:(cc + 1) * qn, 2 * LANES:3 * LANES].astype(F32)
            vwt_ref[cc] = blk.T[hd:, :].astype(BF16)

    q2 = q_ref[...]
    qst = jnp.concatenate([_dot_nt(pselt_ref[r], q2) for r in range(rep)], axis=1).astype(BF16)
    tq = qb * qn + lax.broadcasted_iota(jnp.int32, (1, qn), 1)
    tq4 = jnp.concatenate([tq] * rep, axis=1)

    c0 = jnp.maximum(qb - WINDOW // qn, 0)
    wstart = pl.multiple_of(c0 * qn, qn)
    sw = _dot(kv_ref[pl.ds(wstart, WIN_KEYS), 2 * LANES:3 * LANES], qst)
    sw = sw + wb_ref[jnp.minimum(qb, WINDOW // qn)]

    ncp = kcvc_ref.shape[0]
    s = _dot(kcvc_ref[...], qst)
    n_i = lax.broadcasted_iota(jnp.int32, (ncp, 1), 0)
    p = _masked_softmax0(s, (n_i * CMP_STRIDE + (CMP_BLOCK - 1)) <= tq4)
    o_cmp = _dot(vct_ref[...], p.astype(BF16))
    psum = p[:, 0:qn]
    for r in range(1, rep):
        psum = psum + p[:, r * qn:(r + 1) * qn]
    imp = _dot_f32_rhs(ovt_ref[...], psum)

    ew = jnp.exp(sw - jnp.max(sw, axis=0, keepdims=True))
    lw = jnp.sum(ew, axis=0, keepdims=True)
    pw = ew.astype(BF16)

    j = lax.broadcasted_iota(jnp.int32, (n_sel, 1), 0)
    cur = tq // SEL_BLOCK
    forced = (j == 0) | (j == cur) | (j == cur - 1)
    score = jnp.where(forced, FORCE_VALUE, jnp.where(j * SEL_BLOCK <= tq, imp, MASK_VALUE))
    cnt = jnp.zeros((n_sel, qn), F32)
    for jp in range(n_sel):
        row = score[jp:jp + 1, :]
        earlier = jnp.where(j > jp, 1.0, 0.0)
        cnt = cnt + jnp.where(row > score, 1.0, jnp.where(row == score, earlier, 0.0))
    selbias = jnp.where(cnt < n_top, 0.0, MASK_VALUE)
    selbias = jnp.concatenate([selbias] * rep, axis=1)
    for cc in range(n_sel // blocks_per_chunk):
        sb_ref[cc] = selbias[cc * blocks_per_chunk:(cc + 1) * blocks_per_chunk, :]

    o_win = _dot(vwt_ref[c0], pw[0:qn, :])
    for cc in range(1, WIN_KEYS // qn):
        o_win = o_win + _dot(vwt_ref[c0 + cc], pw[cc * qn:(cc + 1) * qn, :])
    o_win = o_win / lw

    m_ref[...] = jnp.full(m_ref.shape, MASK_VALUE, F32)
    l_ref[...] = jnp.zeros(l_ref.shape, F32)
    acc_ref[...] = jnp.zeros(acc_ref.shape, F32)

    def sel_chunk(c, diagonal):
        start = pl.multiple_of(c * SEL_KC, SEL_KC)
        sc = _dot(kv_ref[pl.ds(start, SEL_KC), LANES:2 * LANES], qst)
        bias = sb_ref[c]
        sc = jnp.concatenate([sc[b * SEL_BLOCK:(b + 1) * SEL_BLOCK, :] + bias[b:b + 1, :]
                              for b in range(blocks_per_chunk)], axis=0)
        if diagonal:
            sc = sc + db_ref[qb % (SEL_KC // qn)]
        m_old = m_ref[...]
        m_new = jnp.maximum(m_old, jnp.max(sc, axis=0, keepdims=True))
        alpha = jnp.exp(m_old - m_new)
        pc = jnp.exp(sc - m_new)
        l_ref[...] = alpha * l_ref[...] + jnp.sum(pc, axis=0, keepdims=True)
        acc_ref[...] = alpha * acc_ref[...] + _dot(vst_ref[c], pc.astype(BF16))
        m_ref[...] = m_new

    last = (qb * qn) // SEL_KC

    def full_chunk(c, carry):
        sel_chunk(c, False)
        return carry

    lax.fori_loop(0, last, full_chunk, 0)
    sel_chunk(last, True)
    o_sel = acc_ref[...] / l_ref[...]

    gs = _sigmoid(gate_ref[...])
    out = None
    for r in range(rep):
        sl = slice(r * qn, (r + 1) * qn)
        o_r = (gs[3 * r:3 * r + 1, :] * o_cmp[:, sl] + gs[3 * r + 1:3 * r + 2, :] * o_sel[:, sl]
               + gs[3 * r + 2:3 * r + 3, :] * o_win[:, sl])
        term = _dot(o_r.T.astype(BF16), qout_ref[r])
        out = term if out is None else out + term
    o_ref[...] = out.astype(o_ref.dtype)


def _nsa(q, kv, kcvc, vct, gates_t, seq):
    t = q.shape[0]
    bsz = t // seq
    nqb = seq // Q_BLOCK
    n_sel = seq // SEL_BLOCK
    n_top = min(SEL_TOP, n_sel)
    ncp = seq // CMP_STRIDE
    assert seq >= WIN_KEYS and seq % SEL_KC == 0 and SEL_KC % Q_BLOCK == 0 and n_sel % 8 == 0
    pselt, qout, overlap_t, win_bias, diag_bias = _nsa_consts(seq)
    gw = NSA_REP * HEAD_DIM
    nq = NSA_REP * Q_BLOCK
    nck = seq // SEL_KC
    body = functools.partial(_nsa_body, n_top=n_top, n_sel=n_sel)
    return pl.pallas_call(
        body,
        grid=(bsz, NSA_GROUPS, nqb),
        in_specs=[pl.BlockSpec((Q_BLOCK, gw), lambda b, g, i: (b * nqb + i, g)),
                  pl.BlockSpec((seq, 3 * LANES), lambda b, g, i: (b, g)),
                  pl.BlockSpec((None, None, ncp, LANES), lambda b, g, i: (b, g, 0, 0)),
                  pl.BlockSpec((None, None, HEAD_DIM, ncp), lambda b, g, i: (b, g, 0, 0)),
                  pl.BlockSpec((None, None, 3 * NSA_REP, Q_BLOCK), lambda b, g, i: (b, g, 0, i)),
                  pl.BlockSpec(pselt.shape, lambda b, g, i: (0, 0, 0)),
                  pl.BlockSpec(qout.shape, lambda b, g, i: (0, 0, 0)),
                  pl.BlockSpec(overlap_t.shape, lambda b, g, i: (0, 0)),
                  pl.BlockSpec(win_bias.shape, lambda b, g, i: (0, 0, 0)),
                  pl.BlockSpec(diag_bias.shape, lambda b, g, i: (0, 0, 0))],
        out_specs=pl.BlockSpec((Q_BLOCK, gw), lambda b, g, i: (b * nqb + i, g)),
        out_shape=jax.ShapeDtypeStruct((t, NSA_HEADS * HEAD_DIM), BF16),
        scratch_shapes=[pltpu.VMEM((nck, HEAD_DIM, SEL_KC), BF16),
                        pltpu.VMEM((nqb, HEAD_DIM, Q_BLOCK), BF16),
                        pltpu.VMEM((nck, SEL_KC // SEL_BLOCK, nq), F32),
                        pltpu.VMEM((1, nq), F32),
                        pltpu.VMEM((1, nq), F32),
                        pltpu.VMEM((HEAD_DIM, nq), F32)],
        compiler_params=_params("arbitrary", "arbitrary", "arbitrary"),
        name="nsa",
    )(q, kv, kcvc, vct, gates_t, pselt, qout, overlap_t, win_bias, diag_bias)


def _dn_body(qkv_ref, sm_ref, z_ref, cw_ref, hp_ref, ng_ref, ones_ref, tri_ref, o_ref,
             state_ref, prev_ref):
    c = pl.program_id(1)
    ch, hd = DN_CHUNK, HEAD_DIM
    width = DN_HEADS * hd
    nb = qkv_ref.shape[0]
    npair = DN_HEADS // 2
    units = [(b, p) for b in range(nb) for p in range(npair)]

    @pl.when(c == 0)
    def _():
        state_ref[...] = jnp.zeros(state_ref.shape, F32)
        prev_ref[...] = jnp.zeros(prev_ref.shape, F32)

    lane = lax.broadcasted_iota(jnp.int32, (1, LANES), 1)
    first = lane < hd
    ri = lax.broadcasted_iota(jnp.int32, (2 * ch, 1), 0)
    ci = lax.broadcasted_iota(jnp.int32, (1, 2 * ch), 1)
    same = (ri // ch) == (ci // ch)
    causal = same & ((ri % ch) >= (ci % ch))
    strict = same & ((ri % ch) > (ci % ch))
    blockdiag = (lax.broadcasted_iota(jnp.int32, (LANES, 1), 0) // hd) == (lane // hd)
    ones_blk = ones_ref[...]

    def stack(v):
        return jnp.concatenate([jnp.where(first, v, 0.0), jnp.where(first, 0.0, v)], axis=0)

    def fold(v):
        return v[:ch] + v[ch:]

    def head_sumsq(v):
        return _dot((v * v).astype(BF16), ones_blk)

    ys, betas, gcs = [], [], []
    for b in range(nb):
        x = qkv_ref[b].astype(F32)
        xe = jnp.concatenate([prev_ref[b], x], axis=0)
        y = cw_ref[DN_CONV - 1:DN_CONV, :] * x
        for tap in range(DN_CONV - 1):
            y = y + cw_ref[tap:tap + 1, :] * pltpu.roll(xe, DN_CONV - 1 - tap, 0)[8:, :]
        prev_ref[b] = x[ch - 8:, :]
        ys.append(y * _sigmoid(y))
        sm = sm_ref[b]
        betas.append(_sigmoid(sm))
        xa = sm + hp_ref[1:2, :]
        softplus = jnp.maximum(xa, 0.0) + jnp.log(1.0 + jnp.exp(-jnp.abs(xa)))
        gcs.append(_dot_f32_rhs(tri_ref[...], -jnp.exp(hp_ref[0:1, :]) * softplus))

    def bc(tile, base, p):
        return jnp.where(first, tile[:, base + 2 * p:base + 2 * p + 1], tile[:, base + 2 * p + 1:base + 2 * p + 2])

    qn, kn, vb, kb, gc, egc, dec = {}, {}, {}, {}, {}, {}, {}
    for u in units:
        b, p = u
        y = ys[b]
        qp = y[:, p * LANES:(p + 1) * LANES]
        kp = y[:, width + p * LANES:width + (p + 1) * LANES]
        vp = y[:, 2 * width + p * LANES:2 * width + (p + 1) * LANES]
        qn[u] = qp * lax.rsqrt(head_sumsq(qp) + EPS) * (hd ** -0.5)
        kn[u] = kp * lax.rsqrt(head_sumsq(kp) + EPS)
        beta = bc(betas[b], SM_BETA, p)
        gc[u] = bc(gcs[b], SM_A, p)
        egc[u] = jnp.exp(gc[u])
        kb[u] = kn[u] * beta
        vb[u] = vp * beta
        gcol = jnp.concatenate([gcs[b][:, SM_A + 2 * p:SM_A + 2 * p + 1],
                                gcs[b][:, SM_A + 2 * p + 1:SM_A + 2 * p + 2]], axis=0)
        gmat = jnp.broadcast_to(gcol, (2 * ch, 2 * ch))
        dec[u] = jnp.where(causal, jnp.exp(jnp.where(causal, gmat - gmat.T, 0.0)), 0.0)

    nmat, attn, sol = {}, {}, {}
    for u in units:
        ks = jnp.concatenate([kn[u], kn[u]], axis=0).astype(BF16)
        nmat[u] = -(_dot_nt(stack(kb[u]).astype(BF16), ks) * jnp.where(strict, dec[u], 0.0))
        attn[u] = (_dot_nt(stack(qn[u]).astype(BF16), ks) * dec[u]).astype(BF16)
        sol[u] = jnp.concatenate([stack(vb[u]), stack(kb[u] * egc[u])], axis=1)

    base = 8
    ii, jj = ri % ch, ci % ch
    eye = jnp.where((ri == ci), 1.0, 0.0)
    tinv = {}
    for u in units:
        n1 = (nmat[u] * jnp.where(same & ((ii // base) == (jj // base)), 1.0, 0.0)).astype(BF16)
        tinv[u] = eye + n1.astype(F32)
        nmat[u] = -nmat[u]
        n2 = _dot(n1, n1).astype(BF16)
        tinv[u] = tinv[u] + _dot(n2, tinv[u].astype(BF16))
        n4 = _dot(n2, n2).astype(BF16)
        tinv[u] = tinv[u] + _dot(n4, tinv[u].astype(BF16))
    s = base
    while s < ch:
        lmask = jnp.where(same & ((ii // (2 * s)) == (jj // (2 * s)))
                          & (((ii // s) % 2) == 1) & (((jj // s) % 2) == 0), 1.0, 0.0)
        for u in units:
            tb = tinv[u].astype(BF16)
            tl = _dot(tb, (nmat[u] * lmask).astype(BF16)).astype(BF16)
            tinv[u] = tinv[u] - _dot(tl, tb)
        s *= 2
    for u in units:
        sol[u] = _dot(tinv[u].astype(BF16), sol[u].astype(BF16))

    v_new, st = {}, {}
    for u in units:
        b, p = u
        st[u] = state_ref[b, p]
        v_new[u] = fold(sol[u][:, :LANES]) - _dot(fold(sol[u][:, LANES:]).astype(BF16), st[u].astype(BF16))

    o = {}
    for u in units:
        b, p = u
        glast = gc[u][ch - 1:ch, :]
        o[u] = (_dot((qn[u] * egc[u]).astype(BF16), st[u].astype(BF16))
                + fold(_dot(attn[u], stack(v_new[u]).astype(BF16))))
        k_dec = kn[u] * jnp.exp(glast - gc[u])
        upd = _dot(k_dec.T.astype(BF16), v_new[u].astype(BF16))
        state_ref[b, p] = st[u] * jnp.exp(glast) + jnp.where(blockdiag, upd, 0.0)

    for u in units:
        b, p = u
        sl = slice(p * LANES, (p + 1) * LANES)
        on = o[u] * lax.rsqrt(head_sumsq(o[u]) * (1.0 / hd) + EPS) * ng_ref[...]
        zp = z_ref[b, :, sl].astype(F32)
        o_ref[b, :, sl] = (on * (zp * _sigmoid(zp))).astype(o_ref.dtype)


def _dn(qkv, sm, z, conv_w, hp, ng, seq):
    t = qkv.shape[0]
    bsz = t // seq
    nb = 2 if bsz % 2 == 0 else 1
    nc = seq // DN_CHUNK
    width = DN_HEADS * HEAD_DIM
    ones_blk = jnp.asarray(np.kron(np.eye(2), np.ones((HEAD_DIM, HEAD_DIM))), BF16)
    tri = jnp.asarray(np.tril(np.ones((DN_CHUNK, DN_CHUNK))), BF16)
    y = pl.pallas_call(
        _dn_body,
        grid=(bsz // nb, nc),
        in_specs=[pl.BlockSpec((nb, DN_CHUNK, 3 * width), lambda b, c: (b, c, 0)),
                  pl.BlockSpec((nb, DN_CHUNK, LANES), lambda b, c: (b, c, 0)),
                  pl.BlockSpec((nb, DN_CHUNK, width), lambda b, c: (b, c, 0)),
                  pl.BlockSpec(conv_w.shape, lambda b, c: (0, 0)),
                  pl.BlockSpec(hp.shape, lambda b, c: (0, 0)),
                  pl.BlockSpec(ng.shape, lambda b, c: (0, 0)),
                  pl.BlockSpec(ones_blk.shape, lambda b, c: (0, 0)),
                  pl.BlockSpec(tri.shape, lambda b, c: (0, 0))],
        out_specs=pl.BlockSpec((nb, DN_CHUNK, width), lambda b, c: (b, c, 0)),
        out_shape=jax.ShapeDtypeStruct((bsz, seq, width), BF16),
        scratch_shapes=[pltpu.VMEM((nb, DN_HEADS // 2, LANES, LANES), F32),
                        pltpu.VMEM((nb, 8, 3 * width), F32)],
        compiler_params=_params("arbitrary", "arbitrary"),
        name="dn",
    )(qkv.reshape(bsz, seq, 3 * width), sm.reshape(bsz, seq, LANES), z.reshape(bsz, seq, width),
      conv_w, hp, ng, ones_blk, tri)
    return y.reshape(t, width)


def _merge_body(yn_ref, yd_ref, mg_ref, x_ref, mod_ref, gf_ref, wb_ref, wo_ref, rwh_ref, rwl_ref,
                rb_ref, tri_ref, x1_ref, h_ref, rt_ref, cnt_ref):
    i = pl.program_id(0)
    d = x_ref.shape[1]

    @pl.when(i == 0)
    def _():
        cnt_ref[...] = jnp.zeros(cnt_ref.shape, F32)

    br0 = _dot(yn_ref[...], wb_ref[0])
    br1 = _dot(yd_ref[...], wb_ref[1])
    mixin = (_sigmoid(mg_ref[:, :d].astype(F32)) * br0 + _sigmoid(mg_ref[:, d:].astype(F32)) * br1)
    mix = _dot(mixin.astype(BF16), wo_ref[...])
    x1 = x_ref[...] + mod_ref[2:3, :] * mix
    x1_ref[...] = x1
    h = _rms_mod(x1, gf_ref[...], mod_ref[3:4, :], mod_ref[4:5, :])
    h_ref[...] = h

    hh = h.astype(BF16)
    hl = (h - hh.astype(F32)).astype(BF16)
    logits = _dot(hh, rwh_ref[...]) + _dot(hh, rwl_ref[...]) + _dot(hl, rwh_ref[...]) + rb_ref[...]
    lane = lax.broadcasted_iota(jnp.int32, (1, LANES), 1)
    cur = jnp.where(lane < N_EXPERTS, logits, PAD_SCORE)
    vals, idxs = [], []
    for _ in range(TOP_K):
        m = jnp.max(cur, axis=-1, keepdims=True)
        ix = jnp.min(jnp.where(cur == m, lane, LANES), axis=-1, keepdims=True)
        vals.append(m)
        idxs.append(ix)
        cur = jnp.where(lane == ix, PAD_SCORE, cur)
    es = [jnp.exp(v - vals[0]) for v in vals]
    den = es[0] + es[1] + es[2] + es[3]

    onehot = jnp.zeros(logits.shape, F32)
    for ix in idxs:
        onehot = onehot + jnp.where(lane == ix, 1.0, 0.0)
    before = _dot(tri_ref[...], onehot.astype(BF16)) + cnt_ref[...]
    cnt_ref[...] = cnt_ref[...] + jnp.sum(onehot, axis=0, keepdims=True)

    rt = jnp.zeros(logits.shape, F32)
    for k in range(TOP_K):
        rank = jnp.sum(jnp.where(lane == idxs[k], before, 0.0), axis=-1, keepdims=True)
        rt = jnp.where(lane == RT_IDX + k, idxs[k].astype(F32), rt)
        rt = jnp.where(lane == RT_W + k, es[k] / den, rt)
        rt = jnp.where(lane == RT_RANK + k, rank, rt)
    rt_ref[...] = rt


def _merge(y_nsa, y_dn, mg, x2, mod3, g_ffn, wb, wo, rwh, rwl, rb, seq):
    t, d = x2.shape
    tm = 512
    per_b = seq // tm
    hw = y_nsa.shape[1]
    tri = jnp.asarray(np.tril(np.ones((tm, tm)), -1), BF16)
    return pl.pallas_call(
        _merge_body,
        grid=(t // tm,),
        in_specs=[pl.BlockSpec((tm, hw), lambda i: (i, 0)),
                  pl.BlockSpec((tm, hw), lambda i: (i, 0)),
                  pl.BlockSpec((tm, 2 * d), lambda i: (i, 0)),
                  pl.BlockSpec((tm, d), lambda i: (i, 0)),
                  pl.BlockSpec((None, 6, d), lambda i: (i // per_b, 0, 0)),
                  pl.BlockSpec((1, d), lambda i: (0, 0)),
                  pl.BlockSpec(wb.shape, lambda i: (0, 0, 0)),
                  pl.BlockSpec(wo.shape, lambda i: (0, 0)),
                  pl.BlockSpec(rwh.shape, lambda i: (0, 0)),
                  pl.BlockSpec(rwl.shape, lambda i: (0, 0)),
                  pl.BlockSpec(rb.shape, lambda i: (0, 0)),
                  pl.BlockSpec(tri.shape, lambda i: (0, 0))],
        out_specs=[pl.BlockSpec((tm, d), lambda i: (i, 0)),
                   pl.BlockSpec((tm, d), lambda i: (i, 0)),
                   pl.BlockSpec((tm, LANES), lambda i: (i, 0)),
                   pl.BlockSpec((1, LANES), lambda i: (0, 0))],
        out_shape=[jax.ShapeDtypeStruct((t, d), F32),
                   jax.ShapeDtypeStruct((t, d), F32),
                   jax.ShapeDtypeStruct((t, LANES), F32),
                   jax.ShapeDtypeStruct((1, LANES), F32)],
        compiler_params=_params("arbitrary"),
        name="merge",
    )(y_nsa, y_dn, mg, x2, mod3, g_ffn.reshape(1, d), wb, wo, rwh, rwl, rb, tri)


def _row_copy(src, src_row, dst, dst_row, sem):
    return pltpu.make_async_copy(src.at[pl.ds(src_row, 1), :], dst.at[pl.ds(dst_row, 1), :], sem)


def _scatter_body(pad_lo_ref, pad_hi_ref, slot_ref, h_ref, xs_ref, zero_ref, sem, zsem):
    tm = h_ref.shape[0]

    @pl.when(pl.program_id(0) == 0)
    def _():
        zero_ref[...] = jnp.zeros(zero_ref.shape, zero_ref.dtype)
        for e in range(N_EXPERTS):
            def zissue(r, carry):
                _row_copy(zero_ref, 0, xs_ref, r, zsem).start()
                return carry

            def zdrain(r, carry):
                _row_copy(zero_ref, 0, xs_ref, r, zsem).wait()
                return carry

            lax.fori_loop(pad_lo_ref[e], pad_hi_ref[e], zissue, 0)
            lax.fori_loop(pad_lo_ref[e], pad_hi_ref[e], zdrain, 0)

    def issue(r, carry):
        for k in range(TOP_K):
            _row_copy(h_ref, r, xs_ref, slot_ref[0, r * TOP_K + k], sem).start(priority=k % 2)
        return carry

    lax.fori_loop(0, tm, issue, 0)

    def drain(r, carry):
        for k in range(TOP_K):
            _row_copy(h_ref, r, xs_ref, slot_ref[0, r * TOP_K + k], sem).wait()
        return carry

    lax.fori_loop(0, tm, drain, 0)


def _scatter(pad_lo, pad_hi, slots3, h, n_rows):
    t, d = h.shape
    tm = SCATTER_TILE
    grid_spec = pltpu.PrefetchScalarGridSpec(
        num_scalar_prefetch=2,
        grid=(t // tm,),
        in_specs=[pl.BlockSpec((None, 1, tm * TOP_K), lambda i, lo, hi: (i, 0, 0), memory_space=pltpu.SMEM),
                  pl.BlockSpec((tm, d), lambda i, lo, hi: (i, 0))],
        out_specs=pl.BlockSpec(memory_space=pl.ANY),
        scratch_shapes=[pltpu.VMEM((8, d), h.dtype), pltpu.SemaphoreType.DMA(()), pltpu.SemaphoreType.DMA(())],
    )
    return pl.pallas_call(
        _scatter_body,
        grid_spec=grid_spec,
        out_shape=jax.ShapeDtypeStruct((n_rows, d), h.dtype),
        compiler_params=_params("arbitrary"),
        name="scatter",
    )(pad_lo, pad_hi, slots3, h)


def _experts_body(te_ref, nv_ref, xs_ref, w1_ref, b1_ref, w2_ref, b2_ref, ys_ref):
    i = pl.program_id(0)
    f = w2_ref.shape[0]

    @pl.when(i < nv_ref[0])
    def _():
        xb = xs_ref[...].astype(BF16)
        u = _dot(xb, w1_ref[...]) + b1_ref[...]
        x_glu = jnp.minimum(u[:, :f], SWIGLU_LIMIT)
        x_lin = jnp.clip(u[:, f:], -SWIGLU_LIMIT, SWIGLU_LIMIT)
        act = x_glu * _sigmoid(SWIGLU_ALPHA * x_glu) * (x_lin + 1.0)
        ys_ref[...] = _dot(act.astype(BF16), w2_ref[...]) + b2_ref[...]

    @pl.when(i >= nv_ref[0])
    def _():
        ys_ref[...] = jnp.zeros(ys_ref.shape, ys_ref.dtype)


def _experts(tile_expert, n_valid, xs, w1, b1, w2, b2):
    p = xs.shape[0]
    d = w1.shape[1]
    tm = EXPERT_TILE
    f = w2.shape[1]
    grid_spec = pltpu.PrefetchScalarGridSpec(
        num_scalar_prefetch=2,
        grid=(p // tm,),
        in_specs=[pl.BlockSpec((tm, xs.shape[1]), lambda i, te, nv: (jnp.minimum(i, nv[0] - 1), 0)),
                  pl.BlockSpec((None, d, 2 * f), lambda i, te, nv: (te[i], 0, 0)),
                  pl.BlockSpec((None, 1, 2 * f), lambda i, te, nv: (te[i], 0, 0)),
                  pl.BlockSpec((None, f, d), lambda i, te, nv: (te[i], 0, 0)),
                  pl.BlockSpec((None, 1, d), lambda i, te, nv: (te[i], 0, 0))],
        out_specs=pl.BlockSpec((tm, d), lambda i, te, nv: (i, 0)),
    )
    return pl.pallas_call(
        _experts_body,
        grid_spec=grid_spec,
        out_shape=jax.ShapeDtypeStruct((p, d), F32),
        compiler_params=_params("arbitrary"),
        name="experts",
    )(tile_expert, n_valid, xs, w1, b1, w2, b2)


def _combine_body(slot_ref, rt_ref, x1_ref, mod_ref, g_ref, ys_ref, o_ref, buf_ref, sem):
    tm = x1_ref.shape[0]

    def issue(r, carry):
        for k in range(TOP_K):
            _row_copy(ys_ref, slot_ref[0, r * TOP_K + k], buf_ref.at[k], r, sem).start(priority=k % 2)
        return carry

    lax.fori_loop(0, tm, issue, 0)

    def drain(r, carry):
        for k in range(TOP_K):
            _row_copy(ys_ref, slot_ref[0, r * TOP_K + k], buf_ref.at[k], r, sem).wait()
        return carry

    lax.fori_loop(0, tm, drain, 0)

    rt = rt_ref[...]
    moe = rt[:, RT_W:RT_W + 1] * buf_ref[0]
    for k in range(1, TOP_K):
        moe = moe + rt[:, RT_W + k:RT_W + k + 1] * buf_ref[k]
    x2 = x1_ref[...] + mod_ref[5:6, :] * moe
    ms = jnp.mean(x2 * x2, axis=-1, keepdims=True)
    o_ref[...] = x2 * lax.rsqrt(ms + EPS) * g_ref[...]


def _combine(slots3, rt, x1, mod3, g_final, ys, seq):
    t, d = x1.shape
    tm = SCATTER_TILE
    per_b = seq // tm
    return pl.pallas_call(
        _combine_body,
        grid=(t // tm,),
        in_specs=[pl.BlockSpec((None, 1, tm * TOP_K), lambda i: (i, 0, 0), memory_space=pltpu.SMEM),
                  pl.BlockSpec((tm, LANES), lambda i: (i, 0)),
                  pl.BlockSpec((tm, d), lambda i: (i, 0)),
                  pl.BlockSpec((None, 6, d), lambda i: (i // per_b, 0, 0)),
                  pl.BlockSpec((1, d), lambda i: (0, 0)),
                  pl.BlockSpec(memory_space=pl.ANY)],
        out_specs=pl.BlockSpec((tm, d), lambda i: (i, 0)),
        out_shape=jax.ShapeDtypeStruct((t, d), F32),
        scratch_shapes=[pltpu.VMEM((TOP_K, tm, ys.shape[1]), ys.dtype), pltpu.SemaphoreType.DMA(())],
        compiler_params=_params("arbitrary"),
        name="combine",
    )(slots3, rt, x1, mod3, g_final.reshape(1, d), ys)


def _pad_lanes(v, offset):
    out = jnp.zeros((1, LANES), F32)
    return out.at[0, offset:offset + v.shape[0]].set(v.astype(F32))


def kernel(x, c, w_ada, b_ada, g_norm_mix, w_in, cmp_pe_k, cmp_pe_v, cmp_w1, cmp_b1, cmp_w2,
           dn_conv_w, dn_a_log, dn_dt_bias, dn_norm_g, w_branch, w_out, g_norm_ffn,
           router_w, router_b, exp_w1, exp_b1, exp_w2, exp_b2, final_norm_g):
    bsz, seq, d = x.shape
    t = bsz * seq
    depth = w_ada.shape[0]
    assert depth == 1, "the final norm is fused into the last layer's combine step"
    x2 = x.reshape(t, d)
    cols = _in_columns()
    out = None
    for l in range(depth):
        mod3 = _ada(c, w_ada[l], b_ada[l]).reshape(bsz, 6, d)

        w_big = jnp.where(jnp.asarray(cols >= 0)[None, :], w_in[l][:, np.maximum(cols, 0)], 0.0).astype(BF16)
        q, kv, dnqkv, z, mg, sm = _inproj(x2, mod3, g_norm_mix[l], w_big, seq)

        nrow = seq // CMP_STRIDE
        src = kv.reshape(bsz, nrow, CMP_STRIDE, NSA_GROUPS, 6, HEAD_DIM)[:, :, :, :, 0:2, :]
        src = src.transpose(0, 3, 4, 1, 2, 5).reshape(bsz, NSA_GROUPS, 2, nrow, CMP_STRIDE * HEAD_DIM)
        pe = jnp.stack([cmp_pe_k[l], cmp_pe_v[l]]).reshape(2, 1, CMP_BLOCK * HEAD_DIM)
        pe = jnp.broadcast_to(pe, (2, 8, CMP_BLOCK * HEAD_DIM)).astype(BF16)
        w2p = jnp.zeros((2, CMP_HIDDEN, LANES), F32)
        w2p = w2p.at[0, :, :HEAD_DIM].set(cmp_w2[l, 0]).at[1, :, HEAD_DIM:].set(cmp_w2[l, 1]).astype(BF16)
        kcvc, vct = _cmp(src, cmp_w1[l].astype(BF16), pe, cmp_b1[l].reshape(2, 1, CMP_HIDDEN), w2p,
                         cmp_w2[l, 1].T.astype(BF16))

        gates_t = sm[:, SM_GATE:SM_GATE + 3 * NSA_HEADS].reshape(bsz, seq, NSA_GROUPS, 3 * NSA_REP)
        gates_t = gates_t.transpose(0, 2, 3, 1)
        y_nsa = _nsa(q, kv, kcvc, vct, gates_t, seq)

        hp = jnp.concatenate([_pad_lanes(dn_a_log[l], SM_A), _pad_lanes(dn_dt_bias[l], SM_A),
                              jnp.zeros((6, LANES), F32)], axis=0)
        ng = jnp.tile(dn_norm_g[l].reshape(1, HEAD_DIM), (1, 2))
        y_dn = _dn(dnqkv, sm, z, dn_conv_w[l], hp, ng, seq)

        rw = jnp.zeros((d, LANES), F32).at[:, :N_EXPERTS].set(router_w[l])
        rwh = rw.astype(BF16)
        rwl = (rw - rwh.astype(F32)).astype(BF16)
        x1, h, rt, cnt = _merge(y_nsa, y_dn, mg, x2, mod3, g_norm_ffn[l], w_branch[l].astype(BF16),
                                w_out[l].astype(BF16), rwh, rwl, _pad_lanes(router_b[l], 0), seq)

        counts = cnt[0, :N_EXPERTS].astype(jnp.int32)
        tiles_per = (counts + EXPERT_TILE - 1) // EXPERT_TILE
        tile_end = jnp.cumsum(tiles_per)
        offs = (tile_end - tiles_per) * EXPERT_TILE
        n_rows = t * TOP_K + N_EXPERTS * EXPERT_TILE
        n_tiles = n_rows // EXPERT_TILE
        idx = rt[:, RT_IDX:RT_IDX + TOP_K].astype(jnp.int32)
        rank = rt[:, RT_RANK:RT_RANK + TOP_K].astype(jnp.int32)
        slots = offs[idx] + rank
        slots3 = slots.reshape(t // SCATTER_TILE, 1, SCATTER_TILE * TOP_K)
        tile_ids = jnp.arange(n_tiles, dtype=jnp.int32)
        tile_expert = jnp.minimum(jnp.sum((tile_ids[:, None] >= tile_end[None, :]).astype(jnp.int32), axis=1),
                                  N_EXPERTS - 1).astype(jnp.int32)
        n_valid = tile_end[-1:].astype(jnp.int32)

        xs = _scatter(offs + counts, offs + tiles_per * EXPERT_TILE, slots3, h, n_rows)
        ys = _experts(tile_expert, n_valid, xs, exp_w1[l].astype(BF16),
                      exp_b1[l].reshape(N_EXPERTS, 1, -1), exp_w2[l].astype(BF16),
                      exp_b2[l].reshape(N_EXPERTS, 1, -1))
        out = _combine(slots3, rt, x1, mod3, final_norm_g, ys, seq)
    return out.reshape(bsz, seq, d)
```

```python
import functools

import numpy as np
import jax
import jax.numpy as jnp
from jax import lax
from jax.experimental import pallas as pl
from jax.experimental.pallas import tpu as pltpu

F32 = jnp.float32
BF16 = jnp.bfloat16

HEAD_DIM = 64
NSA_HEADS = 8
NSA_GROUPS = 2
NSA_REP = NSA_HEADS // NSA_GROUPS
CMP_BLOCK = 32
CMP_STRIDE = 16
CMP_HIDDEN = 256
SEL_BLOCK = 64
SEL_TOP = 16
WINDOW = 512
Q_BLOCK = 256
DN_HEADS = 8
DN_CONV = 4
DN_CHUNK = 64
N_EXPERTS = 32
TOP_K = 4
SWIGLU_LIMIT = 7.0
SWIGLU_ALPHA = 1.702
EPS = 1e-6
MASK_VALUE = -1e30
FORCE_VALUE = 1e9
PAD_SCORE = -3e38

LANES = 128
VMEM_LIMIT = 56 * 2 ** 20

SEL_KC = 512
WIN_KEYS = WINDOW + Q_BLOCK
EXPERT_TILE = 512
SCATTER_TILE = 512

SM_GATE, SM_BETA, SM_A = 0, 24, 32
RT_IDX, RT_W, RT_RANK = 0, 4, 8


def _dot(a, b):
    return jnp.dot(a, b, preferred_element_type=F32)


def _dot_nt(a, b):
    return lax.dot_general(a, b, (((1,), (1,)), ((), ())), preferred_element_type=F32)


def _split3(x):
    hi = x.astype(BF16)
    r1 = x - hi.astype(F32)
    mid = r1.astype(BF16)
    lo = (r1 - mid.astype(F32)).astype(BF16)
    return hi, mid, lo


def _dot_f32_lhs(x, w_bf16):
    hi, mid, lo = _split3(x)
    return _dot(hi, w_bf16) + _dot(mid, w_bf16) + _dot(lo, w_bf16)


def _dot_f32_rhs(w_bf16, x):
    hi, mid, lo = _split3(x)
    return _dot(w_bf16, hi) + _dot(w_bf16, mid) + _dot(w_bf16, lo)


def _sigmoid(x):
    return 1.0 / (1.0 + jnp.exp(-x))


def _params(*sem):
    return pltpu.CompilerParams(dimension_semantics=sem, vmem_limit_bytes=VMEM_LIMIT)


def _ada_body(c_ref, w_ref, b_ref, o_ref):
    c = c_ref[...]
    a = (c * _sigmoid(c)).astype(BF16)
    o_ref[...] = _dot(a, w_ref[...].astype(BF16)) + b_ref[...]


def _ada(c, w, b):
    bsz, d = c.shape
    n = w.shape[1]
    tn = 1024
    return pl.pallas_call(
        _ada_body,
        grid=(n // tn,),
        in_specs=[pl.BlockSpec((bsz, d), lambda j: (0, 0)),
                  pl.BlockSpec((d, tn), lambda j: (0, j)),
                  pl.BlockSpec((1, tn), lambda j: (0, j))],
        out_specs=pl.BlockSpec((bsz, tn), lambda j: (0, j)),
        out_shape=jax.ShapeDtypeStruct((bsz, n), F32),
        compiler_params=_params("arbitrary"),
        name="ada",
    )(c, w, b.reshape(1, n))


IN_SEGS = (("q", 512, BF16), ("kv", 768, BF16), ("dn", 1536, BF16),
           ("z", 512, BF16), ("mg", 2048, BF16), ("sm", LANES, F32))


def _in_columns():
    q0 = 0
    kv0 = q0 + 512
    gate0 = kv0 + 768
    dn0 = gate0 + 24
    beta0 = dn0 + 1536
    a0 = beta0 + 8
    z0 = a0 + 8
    mg0 = z0 + 512
    cols = list(range(q0, q0 + 512))
    for g in range(NSA_GROUPS):
        for i in range(6):
            base = kv0 + i * NSA_GROUPS * HEAD_DIM + g * HEAD_DIM
            cols += list(range(base, base + HEAD_DIM))
    cols += list(range(dn0, dn0 + 1536))
    cols += list(range(z0, z0 + 512))
    cols += list(range(mg0, mg0 + 2048))
    small = list(range(gate0, gate0 + 24)) + list(range(beta0, beta0 + 8)) + list(range(a0, a0 + 8))
    cols += small + [-1] * (LANES - len(small))
    return np.asarray(cols, np.int32)


def _rms_mod(x, g, shift, scale):
    ms = jnp.mean(x * x, axis=-1, keepdims=True)
    y = x * lax.rsqrt(ms + EPS) * g
    return y * (1.0 + scale) + shift


def _inproj_body(x_ref, mod_ref, g_ref, w_ref, *out_refs):
    h = _rms_mod(x_ref[...], g_ref[...], mod_ref[0:1, :], mod_ref[1:2, :])
    hb = h.astype(BF16)
    off = 0
    for ref, (_, width, _) in zip(out_refs, IN_SEGS):
        for c0 in range(0, width, 512):
            cw = min(512, width - c0)
            ref[:, c0:c0 + cw] = _dot(hb, w_ref[:, off + c0:off + c0 + cw]).astype(ref.dtype)
        off += width


def _inproj(x2, mod3, g, w_big, seq):
    t, d = x2.shape
    tm = 512
    per_b = seq // tm
    nw = w_big.shape[1]
    return pl.pallas_call(
        _inproj_body,
        grid=(t // tm,),
        in_specs=[pl.BlockSpec((tm, d), lambda i: (i, 0)),
                  pl.BlockSpec((None, 6, d), lambda i: (i // per_b, 0, 0)),
                  pl.BlockSpec((1, d), lambda i: (0, 0)),
                  pl.BlockSpec((d, nw), lambda i: (0, 0))],
        out_specs=[pl.BlockSpec((tm, w), lambda i: (i, 0)) for _, w, _ in IN_SEGS],
        out_shape=[jax.ShapeDtypeStruct((t, w), dt) for _, w, dt in IN_SEGS],
        compiler_params=_params("arbitrary"),
        name="inproj",
    )(x2, mod3, g.reshape(1, d), w_big)


def _cmp_body(src_ref, w1_ref, pe_ref, b1_ref, w2_ref, w2t_ref, o_ref, ot_ref):
    half = CMP_STRIDE * HEAD_DIM
    out = None
    for kind in range(2):
        x = src_ref[kind]
        first = _dot(x, w1_ref[kind, :half, :])
        second = _dot(x, w1_ref[kind, half:, :])
        n = second.shape[0]
        second = pltpu.roll(second, n - 1, 0)
        pew = _dot(pe_ref[kind], w1_ref[kind])[0:1, :]
        pre = first + second + pew + b1_ref[kind]
        hid = (pre * _sigmoid(pre)).astype(BF16)
        term = _dot(hid, w2_ref[kind])
        out = term if out is None else out + term
    o_ref[...] = out.astype(o_ref.dtype)
    ot_ref[...] = _dot_nt(w2t_ref[...], hid).astype(ot_ref.dtype)


def _cmp(src, w1, pe, b1, w2p, w2t):
    bsz, ng, _, nrow, width = src.shape
    return pl.pallas_call(
        _cmp_body,
        grid=(bsz, ng),
        in_specs=[pl.BlockSpec((None, None, 2, nrow, width), lambda b, g: (b, g, 0, 0, 0)),
                  pl.BlockSpec(w1.shape, lambda b, g: (0, 0, 0)),
                  pl.BlockSpec(pe.shape, lambda b, g: (0, 0, 0)),
                  pl.BlockSpec(b1.shape, lambda b, g: (0, 0, 0)),
                  pl.BlockSpec(w2p.shape, lambda b, g: (0, 0, 0)),
                  pl.BlockSpec(w2t.shape, lambda b, g: (0, 0))],
        out_specs=[pl.BlockSpec((None, None, nrow, LANES), lambda b, g: (b, g, 0, 0)),
                   pl.BlockSpec((None, None, HEAD_DIM, nrow), lambda b, g: (b, g, 0, 0))],
        out_shape=[jax.ShapeDtypeStruct((bsz, ng, nrow, LANES), BF16),
                   jax.ShapeDtypeStruct((bsz, ng, HEAD_DIM, nrow), BF16)],
        compiler_params=_params("arbitrary", "arbitrary"),
        name="cmp",
    )(src, w1, pe, b1, w2p, w2t)


def _nsa_consts(seq):
    scale = HEAD_DIM ** -0.5
    gw = NSA_REP * HEAD_DIM
    pselt = np.zeros((NSA_REP, LANES, gw), np.float32)
    qout = np.zeros((NSA_REP, HEAD_DIM, gw), np.float32)
    for r in range(NSA_REP):
        for d in range(HEAD_DIM):
            pselt[r, d, r * HEAD_DIM + d] = scale
            qout[r, d, r * HEAD_DIM + d] = 1.0
    n_cmp = (seq - CMP_BLOCK) // CMP_STRIDE + 1
    n_sel = seq // SEL_BLOCK
    cs = np.arange(n_cmp)[:, None] * CMP_STRIDE
    ss = np.arange(n_sel)[None, :] * SEL_BLOCK
    ov = np.clip(np.minimum(cs + CMP_BLOCK, ss + SEL_BLOCK) - np.maximum(cs, ss), 0, None) / CMP_BLOCK
    overlap_t = np.zeros((n_sel, seq // CMP_STRIDE), np.float32)
    overlap_t[:, :n_cmp] = ov.T
    qrel = np.tile(np.arange(Q_BLOCK), NSA_REP)[None, None, :]
    shift = (np.arange(WINDOW // Q_BLOCK + 1) * Q_BLOCK)[:, None, None]
    delta = shift + qrel - np.arange(WIN_KEYS)[None, :, None]
    win_bias = np.where((delta >= 0) & (delta < WINDOW), 0.0, MASK_VALUE).astype(np.float32)
    shift = (np.arange(SEL_KC // Q_BLOCK) * Q_BLOCK)[:, None, None]
    diag_bias = np.where(np.arange(SEL_KC)[None, :, None] <= shift + qrel, 0.0, MASK_VALUE).astype(np.float32)
    return (jnp.asarray(pselt, BF16), jnp.asarray(qout, BF16), jnp.asarray(overlap_t, BF16),
            jnp.asarray(win_bias), jnp.asarray(diag_bias))


def _masked_softmax0(s, allowed):
    s = jnp.where(allowed, s, MASK_VALUE)
    e = jnp.exp(s - jnp.max(s, axis=0, keepdims=True))
    p = e / jnp.sum(e, axis=0, keepdims=True)
    return jnp.where(allowed, p, 0.0)


def _nsa_body(q_ref, kv_ref, kcvc_ref, vct_ref, gate_ref, pselt_ref, qout_ref, ovt_ref, wb_ref, db_ref,
              o_ref, vst_ref, vwt_ref, sb_ref, m_ref, l_ref, acc_ref, *, n_top, n_sel):
    qb = pl.program_id(2)
    rep, qn, hd = NSA_REP, Q_BLOCK, HEAD_DIM
    nq = rep * qn
    blocks_per_chunk = SEL_KC // SEL_BLOCK

    @pl.when(qb == 0)
    def _():
        for cc in range(vst_ref.shape[0]):
            blk = kv_ref[cc * SEL_KC:(cc + 1) * SEL_KC, LANES:2 * LANES].astype(F32)
            vst_ref[cc] = blk.T[hd:, :].astype(BF16)
        for cc in range(vwt_ref.shape[0]):
            blk = kv_ref[cc * qn:(cc + 1) * qn, 2 * LANES:3 * LANES].astype(F32)
            vwt_ref[cc] = blk.T[hd:, :].astype(BF16)

    q2 = q_ref[...]
    qst = jnp.concatenate([_dot_nt(pselt_ref[r], q2) for r in range(rep)], axis=1).astype(BF16)
    tq = qb * qn + lax.broadcasted_iota(jnp.int32, (1, qn), 1)
    tq4 = jnp.concatenate([tq] * rep, axis=1)

    c0 = jnp.maximum(qb - WINDOW // qn, 0)
    wstart = pl.multiple_of(c0 * qn, qn)
    sw = _dot(kv_ref[pl.ds(wstart, WIN_KEYS), 2 * LANES:3 * LANES], qst)
    sw = sw + wb_ref[jnp.minimum(qb, WINDOW // qn)]

    ncp = kcvc_ref.shape[0]
    s = _dot(kcvc_ref[...], qst)
    n_i = lax.broadcasted_iota(jnp.int32, (ncp, 1), 0)
    p = _masked_softmax0(s, (n_i * CMP_STRIDE + (CMP_BLOCK - 1)) <= tq4)
    o_cmp = _dot(vct_ref[...], p.astype(BF16))
    psum = p[:, 0:qn]
    for r in range(1, rep):
        psum = psum + p[:, r * qn:(r + 1) * qn]
    imp = _dot_f32_rhs(ovt_ref[...], psum)

    ew = jnp.exp(sw - jnp.max(sw, axis=0, keepdims=True))
    lw = jnp.sum(ew, axis=0, keepdims=True)
    pw = ew.astype(BF16)

    j = lax.broadcasted_iota(jnp.int32, (n_sel, 1), 0)
    cur = tq // SEL_BLOCK
    forced = (j == 0) | (j == cur) | (j == cur - 1)
    score = jnp.where(forced, FORCE_VALUE, jnp.where(j * SEL_BLOCK <= tq, imp, MASK_VALUE))
    cnt = jnp.zeros((n_sel, qn), F32)
    for jp in range(n_sel):
        row = score[jp:jp + 1, :]
        earlier = jnp.where(j > jp, 1.0, 0.0)
        cnt = cnt + jnp.where(row > score, 1.0, jnp.where(row == score, earlier, 0.0))
    selbias = jnp.where(cnt < n_top, 0.0, MASK_VALUE)
    selbias = jnp.concatenate([selbias] * rep, axis=1)
    for cc in range(n_sel // blocks_per_chunk):
        sb_ref[cc] = selbias[cc * blocks_per_chunk:(cc + 1) * blocks_per_chunk, :]

    o_win = _dot(vwt_ref[c0], pw[0:qn, :])
    for cc in range(1, WIN_KEYS // qn):
        o_win = o_win + _dot(vwt_ref[c0 + cc], pw[cc * qn:(cc + 1) * qn, :])
    o_win = o_win / lw

    m_ref[...] = jnp.full(m_ref.shape, MASK_VALUE, F32)
    l_ref[...] = jnp.zeros(l_ref.shape, F32)
    acc_ref[...] = jnp.zeros(acc_ref.shape, F32)

    def sel_chunk(c, diagonal):
        start = pl.multiple_of(c * SEL_KC, SEL_KC)
        sc = _dot(kv_ref[pl.ds(start, SEL_KC), LANES:2 * LANES], qst)
        bias = sb_ref[c]
        sc = jnp.concatenate([sc[b * SEL_BLOCK:(b + 1) * SEL_BLOCK, :] + bias[b:b + 1, :]
                              for b in range(blocks_per_chunk)], axis=0)
        if diagonal:
            sc = sc + db_ref[qb % (SEL_KC // qn)]
        m_old = m_ref[...]
        m_new = jnp.maximum(m_old, jnp.max(sc, axis=0, keepdims=True))
        alpha = jnp.exp(m_old - m_new)
        pc = jnp.exp(sc - m_new)
        l_ref[...] = alpha * l_ref[...] + jnp.sum(pc, axis=0, keepdims=True)
        acc_ref[...] = alpha * acc_ref[...] + _dot(vst_ref[c], pc.astype(BF16))
        m_ref[...] = m_new

    last = (qb * qn) // SEL_KC

    def full_chunk(c, carry):
        sel_chunk(c, False)
        return carry

    lax.fori_loop(0, last, full_chunk, 0)
    sel_chunk(last, True)
    o_sel = acc_ref[...] / l_ref[...]

    gs = _sigmoid(gate_ref[...])
    out = None
    for r in range(rep):
        sl = slice(r * qn, (r + 1) * qn)
        o_r = (gs[3 * r:3 * r + 1, :] * o_cmp[:, sl] + gs[3 * r + 1:3 * r + 2, :] * o_sel[:, sl]
               + gs[3 * r + 2:3 * r + 3, :] * o_win[:, sl])
        term = _dot(o_r.T.astype(BF16), qout_ref[r])
        out = term if out is None else out + term
    o_ref[...] = out.astype(o_ref.dtype)


def _nsa(q, kv, kcvc, vct, gates_t, seq):
    t = q.shape[0]
    bsz = t // seq
    nqb = seq // Q_BLOCK
    n_sel = seq // SEL_BLOCK
    n_top = min(SEL_TOP, n_sel)
    ncp = seq // CMP_STRIDE
    assert seq >= WIN_KEYS and seq % SEL_KC == 0 and SEL_KC % Q_BLOCK == 0 and n_sel % 8 == 0
    pselt, qout, overlap_t, win_bias, diag_bias = _nsa_consts(seq)
    gw = NSA_REP * HEAD_DIM
    nq = NSA_REP * Q_BLOCK
    nck = seq // SEL_KC
    body = functools.partial(_nsa_body, n_top=n_top, n_sel=n_sel)
    return pl.pallas_call(
        body,
        grid=(bsz, NSA_GROUPS, nqb),
        in_specs=[pl.BlockSpec((Q_BLOCK, gw), lambda b, g, i: (b * nqb + i, g)),
                  pl.BlockSpec((seq, 3 * LANES), lambda b, g, i: (b, g)),
                  pl.BlockSpec((None, None, ncp, LANES), lambda b, g, i: (b, g, 0, 0)),
                  pl.BlockSpec((None, None, HEAD_DIM, ncp), lambda b, g, i: (b, g, 0, 0)),
                  pl.BlockSpec((None, None, 3 * NSA_REP, Q_BLOCK), lambda b, g, i: (b, g, 0, i)),
                  pl.BlockSpec(pselt.shape, lambda b, g, i: (0, 0, 0)),
                  pl.BlockSpec(qout.shape, lambda b, g, i: (0, 0, 0)),
                  pl.BlockSpec(overlap_t.shape, lambda b, g, i: (0, 0)),
                  pl.BlockSpec(win_bias.shape, lambda b, g, i: (0, 0, 0)),
                  pl.BlockSpec(diag_bias.shape, lambda b, g, i: (0, 0, 0))],
        out_specs=pl.BlockSpec((Q_BLOCK, gw), lambda b, g, i: (b * nqb + i, g)),
        out_shape=jax.ShapeDtypeStruct((t, NSA_HEADS * HEAD_DIM), BF16),
        scratch_shapes=[pltpu.VMEM((nck, HEAD_DIM, SEL_KC), BF16),
                        pltpu.VMEM((nqb, HEAD_DIM, Q_BLOCK), BF16),
                        pltpu.VMEM((nck, SEL_KC // SEL_BLOCK, nq), F32),
                        pltpu.VMEM((1, nq), F32),
                        pltpu.VMEM((1, nq), F32),
                        pltpu.VMEM((HEAD_DIM, nq), F32)],
        compiler_params=_params("arbitrary", "arbitrary", "arbitrary"),
        name="nsa",
    )(q, kv, kcvc, vct, gates_t, pselt, qout, overlap_t, win_bias, diag_bias)


def _dn_body(qkv_ref, sm_ref, z_ref, cw_ref, hp_ref, ng_ref, ones_ref, tri_ref, o_ref,
             state_ref, prev_ref):
    c = pl.program_id(1)
    ch, hd = DN_CHUNK, HEAD_DIM
    width = DN_HEADS * hd
    nb = qkv_ref.shape[0]
    npair = DN_HEADS // 2
    units = [(b, p) for b in range(nb) for p in range(npair)]

    @pl.when(c == 0)
    def _():
        state_ref[...] = jnp.zeros(state_ref.shape, F32)
        prev_ref[...] = jnp.zeros(prev_ref.shape, F32)

    lane = lax.broadcasted_iota(jnp.int32, (1, LANES), 1)
    first = lane < hd
    ri = lax.broadcasted_iota(jnp.int32, (2 * ch, 1), 0)
    ci = lax.broadcasted_iota(jnp.int32, (1, 2 * ch), 1)
    same = (ri // ch) == (ci // ch)
    causal = same & ((ri % ch) >= (ci % ch))
    strict = same & ((ri % ch) > (ci % ch))
    blockdiag = (lax.broadcasted_iota(jnp.int32, (LANES, 1), 0) // hd) == (lane // hd)
    ones_blk = ones_ref[...]

    def stack(v):
        return jnp.concatenate([jnp.where(first, v, 0.0), jnp.where(first, 0.0, v)], axis=0)

    def fold(v):
        return v[:ch] + v[ch:]

    def head_sumsq(v):
        return _dot((v * v).astype(BF16), ones_blk)

    ys, betas, gcs = [], [], []
    for b in range(nb):
        x = qkv_ref[b].astype(F32)
        xe = jnp.concatenate([prev_ref[b], x], axis=0)
        y = cw_ref[DN_CONV - 1:DN_CONV, :] * x
        for tap in range(DN_CONV - 1):
            y = y + cw_ref[tap:tap + 1, :] * pltpu.roll(xe, DN_CONV - 1 - tap, 0)[8:, :]
        prev_ref[b] = x[ch - 8:, :]
        ys.append(y * _sigmoid(y))
        sm = sm_ref[b]
        betas.append(_sigmoid(sm))
        xa = sm + hp_ref[1:2, :]
        softplus = jnp.maximum(xa, 0.0) + jnp.log(1.0 + jnp.exp(-jnp.abs(xa)))
        gcs.append(_dot_f32_rhs(tri_ref[...], -jnp.exp(hp_ref[0:1, :]) * softplus))

    def bc(tile, base, p):
        return jnp.where(first, tile[:, base + 2 * p:base + 2 * p + 1], tile[:, base + 2 * p + 1:base + 2 * p + 2])

    qn, kn, vb, kb, gc, egc, dec = {}, {}, {}, {}, {}, {}, {}
    for u in units:
        b, p = u
        y = ys[b]
        qp = y[:, p * LANES:(p + 1) * LANES]
        kp = y[:, width + p * LANES:width + (p + 1) * LANES]
        vp = y[:, 2 * width + p * LANES:2 * width + (p + 1) * LANES]
        qn[u] = qp * lax.rsqrt(head_sumsq(qp) + EPS) * (hd ** -0.5)
        kn[u] = kp * lax.rsqrt(head_sumsq(kp) + EPS)
        beta = bc(betas[b], SM_BETA, p)
        gc[u] = bc(gcs[b], SM_A, p)
        egc[u] = jnp.exp(gc[u])
        kb[u] = kn[u] * beta
        vb[u] = vp * beta
        gcol = jnp.concatenate([gcs[b][:, SM_A + 2 * p:SM_A + 2 * p + 1],
                                gcs[b][:, SM_A + 2 * p + 1:SM_A + 2 * p + 2]], axis=0)
        gmat = jnp.broadcast_to(gcol, (2 * ch, 2 * ch))
        dec[u] = jnp.where(causal, jnp.exp(jnp.where(causal, gmat - gmat.T, 0.0)), 0.0)

    nmat, attn, sol = {}, {}, {}
    for u in units:
        ks = jnp.concatenate([kn[u], kn[u]], axis=0).astype(BF16)
        nmat[u] = -(_dot_nt(stack(kb[u]).astype(BF16), ks) * jnp.where(strict, dec[u], 0.0))
        attn[u] = (_dot_nt(stack(qn[u]).astype(BF16), ks) * dec[u]).astype(BF16)
        sol[u] = jnp.concatenate([stack(vb[u]), stack(kb[u] * egc[u])], axis=1)

    base = 8
    ii, jj = ri % ch, ci % ch
    eye = jnp.where((ri == ci), 1.0, 0.0)
    tinv = {}
    for u in units:
        n1 = (nmat[u] * jnp.where(same & ((ii // base) == (jj // base)), 1.0, 0.0)).astype(BF16)
        tinv[u] = eye + n1.astype(F32)
        nmat[u] = -nmat[u]
        n2 = _dot(n1, n1).astype(BF16)
        tinv[u] = tinv[u] + _dot(n2, tinv[u].astype(BF16))
        n4 = _dot(n2, n2).astype(BF16)
        tinv[u] = tinv[u] + _dot(n4, tinv[u].astype(BF16))
    s = base
    while s < ch:
        lmask = jnp.where(same & ((ii // (2 * s)) == (jj // (2 * s)))
                          & (((ii // s) % 2) == 1) & (((jj // s) % 2) == 0), 1.0, 0.0)
        for u in units:
            tb = tinv[u].astype(BF16)
            tl = _dot(tb, (nmat[u] * lmask).astype(BF16)).astype(BF16)
            tinv[u] = tinv[u] - _dot(tl, tb)
        s *= 2
    for u in units:
        sol[u] = _dot(tinv[u].astype(BF16), sol[u].astype(BF16))

    v_new, st = {}, {}
    for u in units:
        b, p = u
        st[u] = state_ref[b, p]
        v_new[u] = fold(sol[u][:, :LANES]) - _dot(fold(sol[u][:, LANES:]).astype(BF16), st[u].astype(BF16))

    o = {}
    for u in units:
        b, p = u
        glast = gc[u][ch - 1:ch, :]
        o[u] = (_dot((qn[u] * egc[u]).astype(BF16), st[u].astype(BF16))
                + fold(_dot(attn[u], stack(v_new[u]).astype(BF16))))
        k_dec = kn[u] * jnp.exp(glast - gc[u])
        upd = _dot(k_dec.T.astype(BF16), v_new[u].astype(BF16))
        state_ref[b, p] = st[u] * jnp.exp(glast) + jnp.where(blockdiag, upd, 0.0)

    for u in units:
        b, p = u
        sl = slice(p * LANES, (p + 1) * LANES)
        on = o[u] * lax.rsqrt(head_sumsq(o[u]) * (1.0 / hd) + EPS) * ng_ref[...]
        zp = z_ref[b, :, sl].astype(F32)
        o_ref[b, :, sl] = (on * (zp * _sigmoid(zp))).astype(o_ref.dtype)


def _dn(qkv, sm, z, conv_w, hp, ng, seq):
    t = qkv.shape[0]
    bsz = t // seq
    nb = 2 if bsz % 2 == 0 else 1
    nc = seq // DN_CHUNK
    width = DN_HEADS * HEAD_DIM
    ones_blk = jnp.asarray(np.kron(np.eye(2), np.ones((HEAD_DIM, HEAD_DIM))), BF16)
    tri = jnp.asarray(np.tril(np.ones((DN_CHUNK, DN_CHUNK))), BF16)
    y = pl.pallas_call(
        _dn_body,
        grid=(bsz // nb, nc),
        in_specs=[pl.BlockSpec((nb, DN_CHUNK, 3 * width), lambda b, c: (b, c, 0)),
                  pl.BlockSpec((nb, DN_CHUNK, LANES), lambda b, c: (b, c, 0)),
                  pl.BlockSpec((nb, DN_CHUNK, width), lambda b, c: (b, c, 0)),
                  pl.BlockSpec(conv_w.shape, lambda b, c: (0, 0)),
                  pl.BlockSpec(hp.shape, lambda b, c: (0, 0)),
                  pl.BlockSpec(ng.shape, lambda b, c: (0, 0)),
                  pl.BlockSpec(ones_blk.shape, lambda b, c: (0, 0)),
                  pl.BlockSpec(tri.shape, lambda b, c: (0, 0))],
        out_specs=pl.BlockSpec((nb, DN_CHUNK, width), lambda b, c: (b, c, 0)),
        out_shape=jax.ShapeDtypeStruct((bsz, seq, width), BF16),
        scratch_shapes=[pltpu.VMEM((nb, DN_HEADS // 2, LANES, LANES), F32),
                        pltpu.VMEM((nb, 8, 3 * width), F32)],
        compiler_params=_params("arbitrary", "arbitrary"),
        name="dn",
    )(qkv.reshape(bsz, seq, 3 * width), sm.reshape(bsz, seq, LANES), z.reshape(bsz, seq, width),
      conv_w, hp, ng, ones_blk, tri)
    return y.reshape(t, width)


def _merge_body(yn_ref, yd_ref, mg_ref, x_ref, mod_ref, gf_ref, wb_ref, wo_ref, rwh_ref, rwl_ref,
                rb_ref, tri_ref, x1_ref, h_ref, rt_ref, cnt_ref):
    i = pl.program_id(0)
    d = x_ref.shape[1]

    @pl.when(i == 0)
    def _():
        cnt_ref[...] = jnp.zeros(cnt_ref.shape, F32)

    br0 = _dot(yn_ref[...], wb_ref[0])
    br1 = _dot(yd_ref[...], wb_ref[1])
    mixin = (_sigmoid(mg_ref[:, :d].astype(F32)) * br0 + _sigmoid(mg_ref[:, d:].astype(F32)) * br1)
    mix = _dot(mixin.astype(BF16), wo_ref[...])
    x1 = x_ref[...] + mod_ref[2:3, :] * mix
    x1_ref[...] = x1
    h = _rms_mod(x1, gf_ref[...], mod_ref[3:4, :], mod_ref[4:5, :])
    h_ref[...] = h

    hh = h.astype(BF16)
    hl = (h - hh.astype(F32)).astype(BF16)
    logits = _dot(hh, rwh_ref[...]) + _dot(hh, rwl_ref[...]) + _dot(hl, rwh_ref[...]) + rb_ref[...]
    lane = lax.broadcasted_iota(jnp.int32, (1, LANES), 1)
    cur = jnp.where(lane < N_EXPERTS, logits, PAD_SCORE)
    vals, idxs = [], []
    for _ in range(TOP_K):
        m = jnp.max(cur, axis=-1, keepdims=True)
        ix = jnp.min(jnp.where(cur == m, lane, LANES), axis=-1, keepdims=True)
        vals.append(m)
        idxs.append(ix)
        cur = jnp.where(lane == ix, PAD_SCORE, cur)
    es = [jnp.exp(v - vals[0]) for v in vals]
    den = es[0] + es[1] + es[2] + es[3]

    onehot = jnp.zeros(logits.shape, F32)
    for ix in idxs:
        onehot = onehot + jnp.where(lane == ix, 1.0, 0.0)
    before = _dot(tri_ref[...], onehot.astype(BF16)) + cnt_ref[...]
    cnt_ref[...] = cnt_ref[...] + jnp.sum(onehot, axis=0, keepdims=True)

    rt = jnp.zeros(logits.shape, F32)
    for k in range(TOP_K):
        rank = jnp.sum(jnp.where(lane == idxs[k], before, 0.0), axis=-1, keepdims=True)
        rt = jnp.where(lane == RT_IDX + k, idxs[k].astype(F32), rt)
        rt = jnp.where(lane == RT_W + k, es[k] / den, rt)
        rt = jnp.where(lane == RT_RANK + k, rank, rt)
    rt_ref[...] = rt


def _merge(y_nsa, y_dn, mg, x2, mod3, g_ffn, wb, wo, rwh, rwl, rb, seq):
    t, d = x2.shape
    tm = 512
    per_b = seq // tm
    hw = y_nsa.shape[1]
    tri = jnp.asarray(np.tril(np.ones((tm, tm)), -1), BF16)
    return pl.pallas_call(
        _merge_body,
        grid=(t // tm,),
        in_specs=[pl.BlockSpec((tm, hw), lambda i: (i, 0)),
                  pl.BlockSpec((tm, hw), lambda i: (i, 0)),
                  pl.BlockSpec((tm, 2 * d), lambda i: (i, 0)),
                  pl.BlockSpec((tm, d), lambda i: (i, 0)),
                  pl.BlockSpec((None, 6, d), lambda i: (i // per_b, 0, 0)),
                  pl.BlockSpec((1, d), lambda i: (0, 0)),
                  pl.BlockSpec(wb.shape, lambda i: (0, 0, 0)),
                  pl.BlockSpec(wo.shape, lambda i: (0, 0)),
                  pl.BlockSpec(rwh.shape, lambda i: (0, 0)),
                  pl.BlockSpec(rwl.shape, lambda i: (0, 0)),
                  pl.BlockSpec(rb.shape, lambda i: (0, 0)),
                  pl.BlockSpec(tri.shape, lambda i: (0, 0))],
        out_specs=[pl.BlockSpec((tm, d), lambda i: (i, 0)),
                   pl.BlockSpec((tm, d), lambda i: (i, 0)),
                   pl.BlockSpec((tm, LANES), lambda i: (i, 0)),
                   pl.BlockSpec((1, LANES), lambda i: (0, 0))],
        out_shape=[jax.ShapeDtypeStruct((t, d), F32),
                   jax.ShapeDtypeStruct((t, d), F32),
                   jax.ShapeDtypeStruct((t, LANES), F32),
                   jax.ShapeDtypeStruct((1, LANES), F32)],
        compiler_params=_params("arbitrary"),
        name="merge",
    )(y_nsa, y_dn, mg, x2, mod3, g_ffn.reshape(1, d), wb, wo, rwh, rwl, rb, tri)


def _row_copy(src, src_row, dst, dst_row, sem):
    return pltpu.make_async_copy(src.at[pl.ds(src_row, 1), :], dst.at[pl.ds(dst_row, 1), :], sem)


def _scatter_body(pad_lo_ref, pad_hi_ref, slot_ref, h_ref, xs_ref, zero_ref, sem, zsem):
    tm = h_ref.shape[0]

    @pl.when(pl.program_id(0) == 0)
    def _():
        zero_ref[...] = jnp.zeros(zero_ref.shape, zero_ref.dtype)
        for e in range(N_EXPERTS):
            def zissue(r, carry):
                _row_copy(zero_ref, 0, xs_ref, r, zsem).start()
                return carry

            def zdrain(r, carry):
                _row_copy(zero_ref, 0, xs_ref, r, zsem).wait()
                return carry

            lax.fori_loop(pad_lo_ref[e], pad_hi_ref[e], zissue, 0)
            lax.fori_loop(pad_lo_ref[e], pad_hi_ref[e], zdrain, 0)

    def issue(r, carry):
        for k in range(TOP_K):
            _row_copy(h_ref, r, xs_ref, slot_ref[0, r * TOP_K + k], sem).start(priority=k % 2)
        return carry

    lax.fori_loop(0, tm, issue, 0)

    def drain(r, carry):
        for k in range(TOP_K):
            _row_copy(h_ref, r, xs_ref, slot_ref[0, r * TOP_K + k], sem).wait()
        return carry

    lax.fori_loop(0, tm, drain, 0)


def _scatter(pad_lo, pad_hi, slots3, h, n_rows):
    t, d = h.shape
    tm = SCATTER_TILE
    grid_spec = pltpu.PrefetchScalarGridSpec(
        num_scalar_prefetch=2,
        grid=(t // tm,),
        in_specs=[pl.BlockSpec((None, 1, tm * TOP_K), lambda i, lo, hi: (i, 0, 0), memory_space=pltpu.SMEM),
                  pl.BlockSpec((tm, d), lambda i, lo, hi: (i, 0))],
        out_specs=pl.BlockSpec(memory_space=pl.ANY),
        scratch_shapes=[pltpu.VMEM((8, d), h.dtype), pltpu.SemaphoreType.DMA(()), pltpu.SemaphoreType.DMA(())],
    )
    return pl.pallas_call(
        _scatter_body,
        grid_spec=grid_spec,
        out_shape=jax.ShapeDtypeStruct((n_rows, d), h.dtype),
        compiler_params=_params("arbitrary"),
        name="scatter",
    )(pad_lo, pad_hi, slots3, h)


def _experts_body(te_ref, nv_ref, xs_ref, w1_ref, b1_ref, w2_ref, b2_ref, ys_ref):
    i = pl.program_id(0)
    f = w2_ref.shape[0]

    @pl.when(i < nv_ref[0])
    def _():
        xb = xs_ref[...].astype(BF16)
        u = _dot(xb, w1_ref[...]) + b1_ref[...]
        x_glu = jnp.minimum(u[:, :f], SWIGLU_LIMIT)
        x_lin = jnp.clip(u[:, f:], -SWIGLU_LIMIT, SWIGLU_LIMIT)
        act = x_glu * _sigmoid(SWIGLU_ALPHA * x_glu) * (x_lin + 1.0)
        ys_ref[...] = _dot(act.astype(BF16), w2_ref[...]) + b2_ref[...]

    @pl.when(i >= nv_ref[0])
    def _():
        ys_ref[...] = jnp.zeros(ys_ref.shape, ys_ref.dtype)


def _experts(tile_expert, n_valid, xs, w1, b1, w2, b2):
    p = xs.shape[0]
    d = w1.shape[1]
    tm = EXPERT_TILE
    f = w2.shape[1]
    grid_spec = pltpu.PrefetchScalarGridSpec(
        num_scalar_prefetch=2,
        grid=(p // tm,),
        in_specs=[pl.BlockSpec((tm, xs.shape[1]), lambda i, te, nv: (jnp.minimum(i, nv[0] - 1), 0)),
                  pl.BlockSpec((None, d, 2 * f), lambda i, te, nv: (te[i], 0, 0)),
                  pl.BlockSpec((None, 1, 2 * f), lambda i, te, nv: (te[i], 0, 0)),
                  pl.BlockSpec((None, f, d), lambda i, te, nv: (te[i], 0, 0)),
                  pl.BlockSpec((None, 1, d), lambda i, te, nv: (te[i], 0, 0))],
        out_specs=pl.BlockSpec((tm, d), lambda i, te, nv: (i, 0)),
    )
    return pl.pallas_call(
        _experts_body,
        grid_spec=grid_spec,
        out_shape=jax.ShapeDtypeStruct((p, d), F32),
        compiler_params=_params("arbitrary"),
        name="experts",
    )(tile_expert, n_valid, xs, w1, b1, w2, b2)


def _combine_body(slot_ref, rt_ref, x1_ref, mod_ref, g_ref, ys_ref, o_ref, buf_ref, sem):
    tm = x1_ref.shape[0]

    def issue(r, carry):
        for k in range(TOP_K):
            _row_copy(ys_ref, slot_ref[0, r * TOP_K + k], buf_ref.at[k], r, sem).start(priority=k % 2)
        return carry

    lax.fori_loop(0, tm, issue, 0)

    def drain(r, carry):
        for k in range(TOP_K):
            _row_copy(ys_ref, slot_ref[0, r * TOP_K + k], buf_ref.at[k], r, sem).wait()
        return carry

    lax.fori_loop(0, tm, drain, 0)

    rt = rt_ref[...]
    moe = rt[:, RT_W:RT_W + 1] * buf_ref[0]
    for k in range(1, TOP_K):
        moe = moe + rt[:, RT_W + k:RT_W + k + 1] * buf_ref[k]
    x2 = x1_ref[...] + mod_ref[5:6, :] * moe
    ms = jnp.mean(x2 * x2, axis=-1, keepdims=True)
    o_ref[...] = x2 * lax.rsqrt(ms + EPS) * g_ref[...]


def _combine(slots3, rt, x1, mod3, g_final, ys, seq):
    t, d = x1.shape
    tm = SCATTER_TILE
    per_b = seq // tm
    return pl.pallas_call(
        _combine_body,
        grid=(t // tm,),
        in_specs=[pl.BlockSpec((None, 1, tm * TOP_K), lambda i: (i, 0, 0), memory_space=pltpu.SMEM),
                  pl.BlockSpec((tm, LANES), lambda i: (i, 0)),
                  pl.BlockSpec((tm, d), lambda i: (i, 0)),
                  pl.BlockSpec((None, 6, d), lambda i: (i // per_b, 0, 0)),
                  pl.BlockSpec((1, d), lambda i: (0, 0)),
                  pl.BlockSpec(memory_space=pl.ANY)],
        out_specs=pl.BlockSpec((tm, d), lambda i: (i, 0)),
        out_shape=jax.ShapeDtypeStruct((t, d), F32),
        scratch_shapes=[pltpu.VMEM((TOP_K, tm, ys.shape[1]), ys.dtype), pltpu.SemaphoreType.DMA(())],
        compiler_params=_params("arbitrary"),
        name="combine",
    )(slots3, rt, x1, mod3, g_final.reshape(1, d), ys)


def _pad_lanes(v, offset):
    out = jnp.zeros((1, LANES), F32)
    return out.at[0, offset:offset + v.shape[0]].set(v.astype(F32))


def kernel(x, c, w_ada, b_ada, g_norm_mix, w_in, cmp_pe_k, cmp_pe_v, cmp_w1, cmp_b1, cmp_w2,
           dn_conv_w, dn_a_log, dn_dt_bias, dn_norm_g, w_branch, w_out, g_norm_ffn,
           router_w, router_b, exp_w1, exp_b1, exp_w2, exp_b2, final_norm_g):
    bsz, seq, d = x.shape
    t = bsz * seq
    depth = w_ada.shape[0]
    assert depth == 1, "the final norm is fused into the last layer's combine step"
    x2 = x.reshape(t, d)
    cols = _in_columns()
    out = None
    for l in range(depth):
        mod3 = _ada(c, w_ada[l], b_ada[l]).reshape(bsz, 6, d)

        w_big = jnp.where(jnp.asarray(cols >= 0)[None, :], w_in[l][:, np.maximum(cols, 0)], 0.0).astype(BF16)
        q, kv, dnqkv, z, mg, sm = _inproj(x2, mod3, g_norm_mix[l], w_big, seq)

        nrow = seq // CMP_STRIDE
        src = kv.reshape(bsz, nrow, CMP_STRIDE, NSA_GROUPS, 6, HEAD_DIM)[:, :, :, :, 0:2, :]
        src = src.transpose(0, 3, 4, 1, 2, 5).reshape(bsz, NSA_GROUPS, 2, nrow, CMP_STRIDE * HEAD_DIM)
        pe = jnp.stack([cmp_pe_k[l], cmp_pe_v[l]]).reshape(2, 1, CMP_BLOCK * HEAD_DIM)
        pe = jnp.broadcast_to(pe, (2, 8, CMP_BLOCK * HEAD_DIM)).astype(BF16)
        w2p = jnp.zeros((2, CMP_HIDDEN, LANES), F32)
        w2p = w2p.at[0, :, :HEAD_DIM].set(cmp_w2[l, 0]).at[1, :, HEAD_DIM:].set(cmp_w2[l, 1]).astype(BF16)
        kcvc, vct = _cmp(src, cmp_w1[l].astype(BF16), pe, cmp_b1[l].reshape(2, 1, CMP_HIDDEN), w2p,
                         cmp_w2[l, 1].T.astype(BF16))

        gates_t = sm[:, SM_GATE:SM_GATE + 3 * NSA_HEADS].reshape(bsz, seq, NSA_GROUPS, 3 * NSA_REP)
        gates_t = gates_t.transpose(0, 2, 3, 1)
        y_nsa = _nsa(q, kv, kcvc, vct, gates_t, seq)

        hp = jnp.concatenate([_pad_lanes(dn_a_log[l], SM_A), _pad_lanes(dn_dt_bias[l], SM_A),
                              jnp.zeros((6, LANES), F32)], axis=0)
        ng = jnp.tile(dn_norm_g[l].reshape(1, HEAD_DIM), (1, 2))
        y_dn = _dn(dnqkv, sm, z, dn_conv_w[l], hp, ng, seq)

        rw = jnp.zeros((d, LANES), F32).at[:, :N_EXPERTS].set(router_w[l])
        rwh = rw.astype(BF16)
        rwl = (rw - rwh.astype(F32)).astype(BF16)
        x1, h, rt, cnt = _merge(y_nsa, y_dn, mg, x2, mod3, g_norm_ffn[l], w_branch[l].astype(BF16),
                                w_out[l].astype(BF16), rwh, rwl, _pad_lanes(router_b[l], 0), seq)

        counts = cnt[0, :N_EXPERTS].astype(jnp.int32)
        tiles_per = (counts + EXPERT_TILE - 1) // EXPERT_TILE
        tile_end = jnp.cumsum(tiles_per)
        offs = (tile_end - tiles_per) * EXPERT_TILE
        n_rows = t * TOP_K + N_EXPERTS * EXPERT_TILE
        n_tiles = n_rows // EXPERT_TILE
        idx = rt[:, RT_IDX:RT_IDX + TOP_K].astype(jnp.int32)
        rank = rt[:, RT_RANK:RT_RANK + TOP_K].astype(jnp.int32)
        slots = offs[idx] + rank
        slots3 = slots.reshape(t // SCATTER_TILE, 1, SCATTER_TILE * TOP_K)
        tile_ids = jnp.arange(n_tiles, dtype=jnp.int32)
        tile_expert = jnp.minimum(jnp.sum((tile_ids[:, None] >= tile_end[None, :]).astype(jnp.int32), axis=1),
                                  N_EXPERTS - 1).astype(jnp.int32)
        n_valid = tile_end[-1:].astype(jnp.int32)

        xs = _scatter(offs + counts, offs + tiles_per * EXPERT_TILE, slots3, h, n_rows)
        ys = _experts(tile_expert, n_valid, xs, exp_w1[l].astype(BF16),
                      exp_b1[l].reshape(N_EXPERTS, 1, -1), exp_w2[l].astype(BF16),
                      exp_b2[l].reshape(N_EXPERTS, 1, -1))
        out = _combine(slots3, rt, x1, mod3, final_norm_g, ys, seq)
    return out.reshape(bsz, seq, d)
```
